```python
import math
import jax, jax.numpy as jnp
from jax import lax
import numpy as np

D_MODEL = 2048
BATCH = 16
SEQ = 2048
DEPTH = 2

A_HEADS = 8
A_HEAD_DIM = 64
A_Q_RANK = 384
A_KV_RANK = 256
IDX_HEADS = 16
IDX_DIM = 64
DSA_TOPK = 256
DSA_Q_BLOCK = 128
SSM_GROUP = 16
SSM_GROUPS = 32
SSM_WIDTH = SSM_GROUP * SSM_GROUPS
SSM_STATE = 64
C_HEADS = 8
C_HEAD_DIM = 64
MOBA_BLOCK = 256
MOBA_TOPK = 3
MOBA_Q_BLOCK = 32
N_BRANCH = 3
BRANCH_WIDTH = 512
D_FF = 256 * math.ceil(8 * D_MODEL / 3 / 256)
CONV_WIDTH = 3
REL_BUCKETS = 32
REL_MAX_DIST = 128
LN_EPS = 1e-5
NEG_INF = -1e30
DEEPNORM_ALPHA = (2 * DEPTH) ** 0.25
DEEPNORM_BETA = (8 * DEPTH) ** -0.25
IN_SPLITS = (A_Q_RANK, A_KV_RANK, IDX_DIM, IDX_HEADS, SSM_WIDTH,
             C_HEADS * C_HEAD_DIM, C_HEADS * C_HEAD_DIM, C_HEADS * C_HEAD_DIM,
             N_BRANCH * D_MODEL)
IN_WIDTH = sum(IN_SPLITS)

kernel_name = 'dsa_s5_moba_gated_hybrid'

F32 = jnp.float32


def _split_points(sizes):
    pts, acc = [], 0
    for s in sizes[:-1]:
        acc += s
        pts.append(acc)
    return pts


def layer_norm(x, g, b):
    xf = x.astype(F32)
    mu = xf.mean(-1, keepdims=True)
    var = jnp.square(xf - mu).mean(-1, keepdims=True)
    return ((xf - mu) * lax.rsqrt(var + LN_EPS) * g.astype(F32) + b.astype(F32)).astype(x.dtype)


def rms_norm(x, g):
    xf = x.astype(F32)
    return (xf * lax.rsqrt(jnp.mean(xf * xf, -1, keepdims=True) + LN_EPS) * g.astype(F32)).astype(x.dtype)


def t5_bucket(dist):
    n = jnp.maximum(dist, 0)
    exact = REL_BUCKETS // 2
    log_ratio = jnp.log(jnp.maximum(n, 1).astype(F32) / exact) / math.log(REL_MAX_DIST / exact)
    large = jnp.minimum(exact + (log_ratio * (REL_BUCKETS - exact)).astype(jnp.int32), REL_BUCKETS - 1)
    return jnp.where(n < exact, n, large)


def to_chunks(a, q):
    return jnp.moveaxis(a.reshape(a.shape[0], a.shape[1] // q, q, *a.shape[2:]), 1, 0)


def from_chunks(a):
    a = jnp.moveaxis(a, 0, 1)
    return a.reshape(a.shape[0], a.shape[1] * a.shape[2], *a.shape[3:])


def dsa_mixer(c_q, c_kv, k_idx, w_idx, w_uq, w_uk, w_uv, w_qidx, bias_tab):
    Bn, T, _ = c_q.shape
    n_top = min(DSA_TOPK, T // 4)
    q = jnp.einsum('btr,rhd->bthd', c_q, w_uq)
    q_lat = jnp.einsum('bthd,chd->bthc', q, w_uk) * A_HEAD_DIM ** -0.5
    q_idx = jnp.einsum('btr,rhd->bthd', c_q, w_qidx) * IDX_DIM ** -0.5
    w_idx = w_idx * IDX_HEADS ** -0.5
    key_pos = jnp.arange(T)

    def attend(args):
        ql, qi, wi, qpos = args
        rel = jax.nn.relu(jnp.einsum('bqhd,bsd->bqhs', qi, k_idx))
        score = jnp.einsum('bqh,bqhs->bqs', wi, rel).astype(F32)
        score = jnp.where(qpos[:, None] >= key_pos[None, :], score, -jnp.inf)
        _, idx = lax.top_k(score, n_top)
        kv = jax.vmap(lambda c, i: c[i])(c_kv, idx)
        dist = qpos[None, :, None] - idx
        logits = (jnp.einsum('bqhc,bqkc->bqhk', ql, kv).astype(F32)
                  + jnp.moveaxis(bias_tab[t5_bucket(dist)].astype(F32), -1, 2))
        logits = jnp.where((dist >= 0)[:, :, None, :], logits, NEG_INF)
        p = jax.nn.softmax(logits, axis=-1).astype(c_kv.dtype)
        return jnp.einsum('bqhk,bqkc->bqhc', p, kv)

    pos = jnp.arange(T).reshape(T // DSA_Q_BLOCK, DSA_Q_BLOCK)
    o_lat = from_chunks(lax.map(attend, (to_chunks(q_lat, DSA_Q_BLOCK), to_chunks(q_idx, DSA_Q_BLOCK),
                                         to_chunks(w_idx, DSA_Q_BLOCK), pos)))
    o = jnp.einsum('bthc,chd->bthd', o_lat, w_uv)
    return o.reshape(Bn, T, A_HEADS * A_HEAD_DIM)


def s5_mixer(u, lam_re, lam_im, log_step, b_re, b_im, c_re, c_im, d_skip, w_glu, b_glu):
    Bn, T, _ = u.shape
    uf = u.astype(F32).reshape(Bn, T, SSM_GROUPS, SSM_GROUP)
    lam = lax.complex(lam_re.astype(F32), lam_im.astype(F32))
    step = jnp.exp(log_step.astype(F32))[:, None]
    lam_bar = jnp.exp(lam * step)
    b_bar = ((lam_bar - 1.0) / lam)[:, :, None] * lax.complex(b_re.astype(F32), b_im.astype(F32))
    bu = jnp.einsum('btgp,gnp->btgn', uf.astype(jnp.complex64), b_bar)
    a = jnp.broadcast_to(lam_bar, (1, T) + lam_bar.shape)

    def combine(l, r):
        return l[0] * r[0], r[0] * l[1] + r[1]

    _, state = lax.associative_scan(combine, (a, bu), axis=1)
    c = lax.complex(c_re.astype(F32), c_im.astype(F32))
    y = jnp.real(jnp.einsum('btgn,gpn->btgp', state, c)) + d_skip.astype(F32).reshape(SSM_GROUPS, SSM_GROUP) * uf
    y = jax.nn.gelu(y.reshape(Bn, T, SSM_WIDTH)).astype(u.dtype)
    return y * jax.nn.sigmoid(y @ w_glu + b_glu)


def moba_mixer(q, k, v, bias_tab):
    Bn, T, H, Dh = q.shape
    n_blk = -(-T // MOBA_BLOCK)
    pad = ((0, 0), (0, n_blk * MOBA_BLOCK - T), (0, 0), (0, 0))
    k_p, v_p = jnp.pad(k, pad), jnp.pad(v, pad)
    k_mean = k_p.astype(F32).reshape(Bn, n_blk, MOBA_BLOCK, H, Dh).mean(2)
    own = jnp.arange(T) // MOBA_BLOCK
    gate = jnp.einsum('bthd,bnhd->bthn', q.astype(F32), k_mean)
    past = jnp.arange(n_blk)[None, :] < own[:, None]
    gate = jnp.where(past[None, :, None, :], gate, -jnp.inf)
    n_sel = min(MOBA_TOPK, n_blk)
    _, sel = lax.top_k(gate, n_sel)
    k_bh = jnp.transpose(k_p.reshape(Bn, n_blk, MOBA_BLOCK, H, Dh), (0, 3, 1, 2, 4))
    v_bh = jnp.transpose(v_p.reshape(Bn, n_blk, MOBA_BLOCK, H, Dh), (0, 3, 1, 2, 4))
    scale = Dh ** -0.5
    tab_t = bias_tab.astype(F32).T
    head_ix = jnp.arange(H)[None, :, None, None, None]
    gather = jax.vmap(jax.vmap(lambda blocks, s: blocks[s]))

    def attend(args):
        qc, selc, start = args
        qpos = start + jnp.arange(MOBA_Q_BLOCK)
        own_blk = start // MOBA_BLOCK
        ob = own_blk * MOBA_BLOCK
        k_own = lax.dynamic_slice_in_dim(k_p, ob, MOBA_BLOCK, axis=1)
        v_own = lax.dynamic_slice_in_dim(v_p, ob, MOBA_BLOCK, axis=1)
        d_own = qpos[:, None] - (ob + jnp.arange(MOBA_BLOCK))[None, :]
        l_own = jnp.einsum('bqhd,bshd->bhqs', qc, k_own).astype(F32) * scale + tab_t[:, t5_bucket(d_own)]
        l_own = jnp.where(d_own >= 0, l_own, NEG_INF)
        sel_bh = jnp.transpose(selc, (0, 2, 1, 3))
        k_sel = gather(k_bh, sel_bh)
        v_sel = gather(v_bh, sel_bh)
        d_sel = qpos[:, None, None] - (sel_bh[..., None] * MOBA_BLOCK + jnp.arange(MOBA_BLOCK))
        l_sel = (jnp.einsum('bqhd,bhqjsd->bhqjs', qc, k_sel).astype(F32) * scale
                 + tab_t[head_ix, t5_bucket(d_sel)])
        l_sel = jnp.where((sel_bh < own_blk)[..., None], l_sel, NEG_INF)
        logits = jnp.concatenate([l_own, l_sel.reshape(Bn, H, MOBA_Q_BLOCK, n_sel * MOBA_BLOCK)], axis=-1)
        p = jax.nn.softmax(logits, axis=-1).astype(v.dtype)
        p_own = p[..., :MOBA_BLOCK]
        p_sel = p[..., MOBA_BLOCK:].reshape(Bn, H, MOBA_Q_BLOCK, n_sel, MOBA_BLOCK)
        return (jnp.einsum('bhqs,bshd->bqhd', p_own, v_own)
                + jnp.einsum('bhqjs,bhqjsd->bqhd', p_sel, v_sel))

    starts = jnp.arange(T // MOBA_Q_BLOCK) * MOBA_Q_BLOCK
    o = from_chunks(lax.map(attend, (to_chunks(q, MOBA_Q_BLOCK), to_chunks(sel, MOBA_Q_BLOCK), starts)))
    return o.reshape(Bn, T, H * Dh)


def mixer_block(x, rel_bias, w_in, cq_gain, ckv_gain, w_uq, w_uk, w_uv, w_qidx,
                lam_re, lam_im, log_step, b_re, b_im, c_re, c_im, d_skip, w_glu, b_glu,
                w_branch, w_out):
    Bn, T, _ = x.shape
    h = jnp.einsum('btd,dn->btn', x, w_in)
    c_q, c_kv, k_idx, w_idx, u, q_c, k_c, v_c, g = jnp.split(h, _split_points(IN_SPLITS), axis=-1)
    o_a = dsa_mixer(rms_norm(c_q, cq_gain), rms_norm(c_kv, ckv_gain), k_idx, w_idx,
                    w_uq, w_uk, w_uv, w_qidx, rel_bias[:, :A_HEADS])
    o_b = s5_mixer(u, lam_re, lam_im, log_step, b_re, b_im, c_re, c_im, d_skip, w_glu, b_glu)
    head_shape = (Bn, T, C_HEADS, C_HEAD_DIM)
    o_c = moba_mixer(q_c.reshape(head_shape), k_c.reshape(head_shape), v_c.reshape(head_shape),
                     rel_bias[:, A_HEADS:])
    o = jnp.stack([o_a, o_b, o_c], axis=2)
    y = jnp.einsum('btnc,ncd->btnd', o, w_branch)
    gate = jax.nn.sigmoid(g.reshape(Bn, T, N_BRANCH, D_MODEL))
    merged = jnp.einsum('btnd,btnd->btd', gate, y)
    return merged @ w_out


def conv_ffn(x, w_up, conv_w, conv_b, w_down):
    T = x.shape[1]
    h = x @ w_up
    hp = jnp.pad(h, ((0, 0), (CONV_WIDTH - 1, 0), (0, 0)))
    h = conv_b + sum(hp[:, j:j + T] * conv_w[j] for j in range(CONV_WIDTH))
    a, val = jnp.split(h, 2, axis=-1)
    return (jax.nn.gelu(a) * val) @ w_down


def setup_inputs(seed: int = 0) -> dict:
    key = jax.random.key(seed)
    ks = iter(jax.random.split(key, 40))

    def nrm(shape, scale):
        return jax.random.normal(next(ks), shape, F32) * scale

    L = DEPTH
    beta = DEEPNORM_BETA
    x = nrm((BATCH, SEQ, D_MODEL), 1.0)
    rel_bias = nrm((REL_BUCKETS, A_HEADS + C_HEADS), 0.2)
    w_in = nrm((L, D_MODEL, IN_WIDTH), D_MODEL ** -0.5)
    cq_gain = 1.0 + nrm((L, A_Q_RANK), 0.01)
    ckv_gain = 1.0 + nrm((L, A_KV_RANK), 0.01)
    w_uq = nrm((L, A_Q_RANK, A_HEADS, A_HEAD_DIM), A_Q_RANK ** -0.5)
    w_uk = nrm((L, A_KV_RANK, A_HEADS, A_HEAD_DIM), A_KV_RANK ** -0.5)
    w_uv = nrm((L, A_KV_RANK, A_HEADS, A_HEAD_DIM), beta * A_KV_RANK ** -0.5)
    w_qidx = nrm((L, A_Q_RANK, IDX_HEADS, IDX_DIM), A_Q_RANK ** -0.5)
    lam_re = -0.5 + nrm((L, SSM_GROUPS, SSM_STATE), 0.01)
    lam_im = math.pi * jnp.arange(SSM_STATE, dtype=F32) + nrm((L, SSM_GROUPS, SSM_STATE), 0.01)
    log_step = jax.random.uniform(next(ks), (L, SSM_GROUPS), F32, math.log(1e-3), math.log(1e-1))
    b_re = nrm((L, SSM_GROUPS, SSM_STATE, SSM_GROUP), (2 * SSM_GROUP) ** -0.5)
    b_im = nrm((L, SSM_GROUPS, SSM_STATE, SSM_GROUP), (2 * SSM_GROUP) ** -0.5)
    c_re = nrm((L, SSM_GROUPS, SSM_GROUP, SSM_STATE), 0.5)
    c_im = nrm((L, SSM_GROUPS, SSM_GROUP, SSM_STATE), 0.5)
    d_skip = nrm((L, SSM_WIDTH), 1.0)
    w_glu = nrm((L, SSM_WIDTH, SSM_WIDTH), SSM_WIDTH ** -0.5)
    b_glu = nrm((L, SSM_WIDTH), 0.01)
    w_branch = nrm((L, N_BRANCH, BRANCH_WIDTH, D_MODEL), beta * BRANCH_WIDTH ** -0.5)
    w_out = nrm((L, D_MODEL, D_MODEL), beta * D_MODEL ** -0.5)
    ln1_g = 1.0 + nrm((L, D_MODEL), 0.01)
    ln1_b = nrm((L, D_MODEL), 0.01)
    w_up = nrm((L, D_MODEL, 2 * D_FF), beta * D_MODEL ** -0.5)
    conv_w = nrm((L, CONV_WIDTH, 2 * D_FF), CONV_WIDTH ** -0.5)
    conv_b = nrm((L, 2 * D_FF), 0.01)
    w_down = nrm((L, D_FF, D_MODEL), beta * D_FF ** -0.5)
    ln2_g = 1.0 + nrm((L, D_MODEL), 0.01)
    ln2_b = nrm((L, D_MODEL), 0.01)
    return {'x': x, 'rel_bias': rel_bias, 'w_in': w_in, 'cq_gain': cq_gain, 'ckv_gain': ckv_gain,
            'w_uq': w_uq, 'w_uk': w_uk, 'w_uv': w_uv, 'w_qidx': w_qidx,
            'lam_re': lam_re, 'lam_im': lam_im, 'log_step': log_step, 'b_re': b_re, 'b_im': b_im,
            'c_re': c_re, 'c_im': c_im, 'd_skip': d_skip, 'w_glu': w_glu, 'b_glu': b_glu,
            'w_branch': w_branch, 'w_out': w_out, 'ln1_g': ln1_g, 'ln1_b': ln1_b,
            'w_up': w_up, 'conv_w': conv_w, 'conv_b': conv_b, 'w_down': w_down,
            'ln2_g': ln2_g, 'ln2_b': ln2_b}


def reference(x, rel_bias, w_in, cq_gain, ckv_gain, w_uq, w_uk, w_uv, w_qidx,
              lam_re, lam_im, log_step, b_re, b_im, c_re, c_im, d_skip, w_glu, b_glu,
              w_branch, w_out, ln1_g, ln1_b, w_up, conv_w, conv_b, w_down, ln2_g, ln2_b):
    for l in range(DEPTH):
        mix = mixer_block(x, rel_bias, w_in[l], cq_gain[l], ckv_gain[l], w_uq[l], w_uk[l], w_uv[l], w_qidx[l],
                          lam_re[l], lam_im[l], log_step[l], b_re[l], b_im[l], c_re[l], c_im[l],
                          d_skip[l], w_glu[l], b_glu[l], w_branch[l], w_out[l])
        x = layer_norm(DEEPNORM_ALPHA * x + mix, ln1_g[l], ln1_b[l])
        x = layer_norm(DEEPNORM_ALPHA * x + conv_ffn(x, w_up[l], conv_w[l], conv_b[l], w_down[l]),
                       ln2_g[l], ln2_b[l])
    return x
```

```python
import functools
import math

import numpy as np
import jax
import jax.numpy as jnp
from jax import lax
from jax.experimental import pallas as pl
from jax.experimental.pallas import tpu as pltpu

F32 = jnp.float32
BF16 = jnp.bfloat16
I32 = jnp.int32

DEPTH = 2
A_HEADS = 8
A_HEAD_DIM = 64
A_Q_RANK = 384
A_KV_RANK = 256
IDX_HEADS = 16
IDX_DIM = 64
DSA_TOPK = 256
SSM_GROUP = 16
SSM_GROUPS = 32
SSM_WIDTH = SSM_GROUP * SSM_GROUPS
SSM_STATE = 64
C_HEADS = 8
C_HEAD_DIM = 64
MOBA_BLOCK = 256
MOBA_TOPK = 3
N_BRANCH = 3
BRANCH_WIDTH = 512
CONV_WIDTH = 3
REL_BUCKETS = 32
REL_MAX_DIST = 128
LN_EPS = 1e-5
NEG_INF = -1e30
DEEPNORM_ALPHA = (2 * DEPTH) ** 0.25

LANES = 128
SUBLANES = 8
VMEM_LIMIT = 56 * 1024 * 1024

CQ_OFF, CKV_OFF, KW_OFF, U_OFF, Q_OFF, K_OFF, V_OFF, PROJ_W = 0, 512, 768, 1024, 1536, 2048, 2560, 3072

DSA_QB = 128
DSA_KC = 256
S5_L = 256
NS = SSM_GROUPS * SSM_STATE


def _cparams(sem):
    return pltpu.CompilerParams(dimension_semantics=sem, vmem_limit_bytes=VMEM_LIMIT)


def _dot(a, b):
    return jnp.dot(a, b, preferred_element_type=F32)


def _dot_nt(a, b):
    return lax.dot_general(a, b, (((1,), (1,)), ((), ())), preferred_element_type=F32)


def _layer_norm(z, g, b):
    mu = jnp.mean(z, axis=-1, keepdims=True)
    zc = z - mu
    var = jnp.mean(zc * zc, axis=-1, keepdims=True)
    return zc * lax.rsqrt(var + LN_EPS) * g + b


def _proj_kernel(x_ref, w_ref, o_ref, xb_ref):
    @pl.when(pl.program_id(1) == 0)
    def _():
        xb_ref[...] = x_ref[...].astype(BF16)

    o_ref[...] = _dot(xb_ref[...], w_ref[...])


def _proj(x2d, w, tm=1024, tn=512):
    n, d = x2d.shape
    pw = w.shape[1]
    tm = min(tm, n)
    return pl.pallas_call(
        _proj_kernel,
        grid=(n // tm, pw // tn),
        in_specs=[pl.BlockSpec((tm, d), lambda i, j: (i, 0)),
                  pl.BlockSpec((d, tn), lambda i, j: (0, j))],
        out_specs=pl.BlockSpec((tm, tn), lambda i, j: (i, j)),
        out_shape=jax.ShapeDtypeStruct((n, pw), F32),
        scratch_shapes=[pltpu.VMEM((tm, d), BF16)],
        compiler_params=_cparams(("parallel", "arbitrary")),
        name="proj",
    )(x2d, w)


def _bias_kernel(bkt_ref, tab_ref, o_ref):
    b = bkt_ref[...]
    acc = jnp.zeros(o_ref.shape, F32)
    for k in range(REL_BUCKETS):
        acc = jnp.where(b == k, tab_ref[:, k:k + 1], acc)
    o_ref[...] = acc


def _bias_expand(bucket, tab_t, pc=8192):
    p = bucket.shape[1]
    h = tab_t.shape[0]
    return pl.pallas_call(
        _bias_kernel,
        grid=(p // pc,),
        in_specs=[pl.BlockSpec((1, pc), lambda i: (0, i)),
                  pl.BlockSpec((h, REL_BUCKETS), lambda i: (0, 0))],
        out_specs=pl.BlockSpec((h, pc), lambda i: (0, i)),
        out_shape=jax.ShapeDtypeStruct((h, p), F32),
        compiler_params=_cparams(("parallel",)),
        name="bias_expand",
    )(bucket, tab_t)


def _t5_bucket_np(dist):
    n = np.maximum(dist, 0)
    exact = REL_BUCKETS // 2
    log_ratio = np.log(np.maximum(n, 1).astype(np.float32) / exact) / math.log(REL_MAX_DIST / exact)
    large = np.minimum(exact + (log_ratio * (REL_BUCKETS - exact)).astype(np.int32), REL_BUCKETS - 1)
    return np.where(n < exact, n, large).astype(np.int32)


def _toeplitz_buckets(rows, cols, deltas):
    i = np.arange(rows)[:, None]
    j = np.arange(cols)[None, :]
    tiles = [_t5_bucket_np(d + i - j) for d in deltas]
    tiles.append(np.full((rows, cols), REL_BUCKETS - 1, np.int32))
    return np.stack(tiles).reshape(1, -1)


def _dsa_kernel(cq_ref, ckv_ref, kw_ref, cqg_ref, ckvg_ref, wuq_ref, wuk_ref, wqi_ref, wuv_ref, bias_ref,
                o_ref, kvn_ref, kk_ref, key_ref, ql_ref, m_ref, l_ref, acc_ref, *, seq, n_top):
    qb = pl.program_id(1)
    q0 = qb * DSA_QB
    n_chunk_rows = seq // DSA_KC

    @pl.when(qb == 0)
    def _():
        lane = lax.broadcasted_iota(I32, (DSA_KC, LANES), 1)

        def prep(c, carry):
            rows = pl.ds(pl.multiple_of(c * DSA_KC, DSA_KC), DSA_KC)
            ckv = ckv_ref[rows, :]
            ms = jnp.mean(ckv * ckv, axis=-1, keepdims=True)
            kvn_ref[rows, :] = (ckv * lax.rsqrt(ms + LN_EPS) * ckvg_ref[...]).astype(BF16)
            kw = kw_ref[rows, :]
            kk_ref[rows, :] = jnp.where(lane < IDX_DIM, kw, pltpu.roll(kw, IDX_DIM, 1)).astype(BF16)
            return carry

        lax.fori_loop(0, n_chunk_rows, prep, 0)

    nc = (q0 + DSA_QB + DSA_KC - 1) // DSA_KC

    cq = cq_ref[...]
    ms = jnp.mean(cq * cq, axis=-1, keepdims=True)
    cqn = (cq * lax.rsqrt(ms + LN_EPS) * cqg_ref[...]).astype(BF16)

    lane_q = lax.broadcasted_iota(I32, (DSA_QB, LANES), 1)
    lo_half = lane_q < IDX_DIM
    row_i = lax.broadcasted_iota(I32, (DSA_QB, DSA_KC), 0)
    col_j = lax.broadcasted_iota(I32, (DSA_QB, DSA_KC), 1)
    qpos = q0 + row_i

    qi = _dot(cqn, wqi_ref[...]) * (IDX_DIM ** -0.5)
    wq = kw_ref[pl.ds(pl.multiple_of(q0, DSA_QB), DSA_QB), :] * (IDX_HEADS ** -0.5)
    q_heads = []
    for h in range(IDX_HEADS):
        qp = qi[:, (h // 2) * LANES:(h // 2 + 1) * LANES]
        keep = lo_half if h % 2 == 0 else jnp.logical_not(lo_half)
        q_heads.append(jnp.where(keep, qp, 0.0).astype(BF16))

    def score_body(c, carry):
        kc = kk_ref[pl.ds(pl.multiple_of(c * DSA_KC, DSA_KC), DSA_KC), :]
        s = jnp.zeros((DSA_QB, DSA_KC), F32)
        for h in range(IDX_HEADS):
            r = _dot_nt(q_heads[h], kc)
            s = s + wq[:, IDX_DIM + h:IDX_DIM + h + 1] * jnp.maximum(r, 0.0)
        s = jnp.where(qpos >= c * DSA_KC + col_j, s, -jnp.inf)
        bits = lax.bitcast_convert_type(s, I32)
        key_ref[c] = bits ^ ((bits >> 31) & 0x7FFFFFFF)
        return carry

    lax.fori_loop(0, nc, score_body, 0)

    def count(pred):
        def body(c, tot):
            one = jnp.where(pred(key_ref[c], c), 1.0, 0.0)
            return tot + one[:, :LANES] + one[:, LANES:]

        tot = lax.fori_loop(0, nc, body, jnp.zeros((DSA_QB, LANES), F32))
        return jnp.sum(tot, axis=1, keepdims=True)

    int_min = jnp.int32(-2 ** 31)
    k_top = float(n_top)
    cnt = count(lambda k, c: k >= 0)
    thr0 = jnp.where(cnt >= k_top, jnp.int32(0), int_min)

    def bit_body(i, thr):
        cand = thr | jnp.left_shift(jnp.int32(1), 30 - i)
        cnt = count(lambda k, c: k >= cand)
        return jnp.where(cnt >= k_top, cand, thr)

    thr = lax.fori_loop(0, 31, bit_body, thr0)

    need = k_top - count(lambda k, c: k > thr)
    n_eq = count(lambda k, c: k == thr)
    any_tie = jnp.max(jnp.where(n_eq > need, 1.0, 0.0)) > 0.0
    idx_bits = int(math.log2(seq))

    def tie_cut():
        def body(i, v):
            cand = v | jnp.left_shift(jnp.int32(1), idx_bits - 1 - i)
            cnt = count(lambda k, c: (k == thr) & (c * DSA_KC + col_j < cand))
            return jnp.where(cnt < need, cand, v)

        return lax.fori_loop(0, idx_bits, body, jnp.zeros((DSA_QB, 1), I32))

    cut = lax.cond(any_tie, tie_cut, lambda: jnp.full((DSA_QB, 1), seq, I32))

    q = _dot(cqn, wuq_ref[...])
    for h in range(A_HEADS):
        qp = q[:, (h // 2) * LANES:(h // 2 + 1) * LANES]
        keep = lo_half if h % 2 == 0 else jnp.logical_not(lo_half)
        qm = jnp.where(keep, qp, 0.0).astype(BF16)
        ql_ref[h] = (_dot(qm, wuk_ref[h // 2]) * (A_HEAD_DIM ** -0.5)).astype(BF16)
    m_ref[...] = jnp.full(m_ref.shape, NEG_INF, F32)
    l_ref[...] = jnp.zeros(l_ref.shape, F32)
    acc_ref[...] = jnp.zeros(acc_ref.shape, F32)

    def attn_body(c, carry):
        kv = kvn_ref[pl.ds(pl.multiple_of(c * DSA_KC, DSA_KC), DSA_KC), :]
        k = key_ref[c]
        kpos = c * DSA_KC + col_j
        sel = ((k > thr) | ((k == thr) & (kpos <= cut))) & (qpos >= kpos)
        t = jnp.minimum((q0 - c * DSA_KC) // DSA_QB, 3)
        for h in range(A_HEADS):
            s = _dot_nt(ql_ref[h], kv) + bias_ref[t, h]
            m_old = m_ref[h]
            m_new = jnp.maximum(m_old, jnp.max(jnp.where(sel, s, NEG_INF), axis=1, keepdims=True))
            a = jnp.exp(m_old - m_new)
            p = jnp.where(sel, jnp.exp(s - m_new), 0.0)
            l_ref[h] = a * l_ref[h] + jnp.sum(p, axis=1, keepdims=True)
            acc_ref[h] = a * acc_ref[h] + _dot(p.astype(BF16), kv)
            m_ref[h] = m_new
        return carry

    lax.fori_loop(0, nc, attn_body, 0)

    for pr in range(A_HEADS // 2):
        o = jnp.zeros((DSA_QB, LANES), F32)
        for h in (2 * pr, 2 * pr + 1):
            o_lat = (acc_ref[h] / l_ref[h]).astype(BF16)
            o = o + _dot(o_lat, wuv_ref[h])
        o_ref[:, pr * LANES:(pr + 1) * LANES] = o.astype(o_ref.dtype)


def _dsa(hm, cq_gain, ckv_gain, w_uq, w_uk, w_uv, w_qidx, bias_tiles):
    bsz, seq, _ = hm.shape
    n_top = min(DSA_TOPK, seq // 4)
    hd = A_HEADS * A_HEAD_DIM
    wuq = w_uq.reshape(A_Q_RANK, hd).astype(BF16)
    wuk = jnp.transpose(w_uk, (1, 2, 0)).reshape(A_HEADS // 2, 2 * A_HEAD_DIM, A_KV_RANK).astype(BF16)
    wqi = w_qidx.reshape(A_Q_RANK, IDX_HEADS * IDX_DIM).astype(BF16)
    wv = jnp.transpose(w_uv, (1, 0, 2))
    zeros = jnp.zeros_like(wv)
    even = (jnp.arange(A_HEADS) % 2 == 0)[:, None, None]
    wuv = jnp.concatenate([jnp.where(even, wv, zeros), jnp.where(even, zeros, wv)], axis=-1).astype(BF16)
    kernel = functools.partial(_dsa_kernel, seq=seq, n_top=n_top)
    const2 = lambda b, q: (0, 0)
    const3 = lambda b, q: (0, 0, 0)
    return pl.pallas_call(
        kernel,
        grid=(bsz, seq // DSA_QB),
        in_specs=[
            pl.BlockSpec((None, DSA_QB, A_Q_RANK), lambda b, q: (b, q, CQ_OFF // A_Q_RANK)),
            pl.BlockSpec((None, seq, A_KV_RANK), lambda b, q: (b, 0, CKV_OFF // A_KV_RANK)),
            pl.BlockSpec((None, seq, LANES), lambda b, q: (b, 0, KW_OFF // LANES)),
            pl.BlockSpec((1, A_Q_RANK), const2),
            pl.BlockSpec((1, A_KV_RANK), const2),
            pl.BlockSpec(wuq.shape, const2),
            pl.BlockSpec(wuk.shape, const3),
            pl.BlockSpec(wqi.shape, const2),
            pl.BlockSpec(wuv.shape, const3),
            pl.BlockSpec(bias_tiles.shape, lambda b, q: (0, 0, 0, 0)),
        ],
        out_specs=pl.BlockSpec((None, DSA_QB, hd), lambda b, q: (b, q, 0)),
        out_shape=jax.ShapeDtypeStruct((bsz, seq, hd), BF16),
        scratch_shapes=[
            pltpu.VMEM((seq, A_KV_RANK), BF16),
            pltpu.VMEM((seq, LANES), BF16),
            pltpu.VMEM((seq // DSA_KC, DSA_QB, DSA_KC), I32),
            pltpu.VMEM((A_HEADS, DSA_QB, A_KV_RANK), BF16),
            pltpu.VMEM((A_HEADS, DSA_QB, 1), F32),
            pltpu.VMEM((A_HEADS, DSA_QB, 1), F32),
            pltpu.VMEM((A_HEADS, DSA_QB, A_KV_RANK), F32),
        ],
        compiler_params=_cparams(("parallel", "arbitrary")),
        name="dsa",
    )(hm, hm, hm, cq_gain.reshape(1, -1), ckv_gain.reshape(1, -1), wuq, wuk, wqi, wuv, bias_tiles)


S5_POWERS = (1, 2, 4) + tuple(range(1, SUBLANES + 1))
S5_ROWS = 2 * len(S5_POWERS) + 2


def _s5_prep_kernel(lre_ref, lim_ref, ls_ref, bre_ref, bim_ref, tab_ref, bbre_ref, bbim_ref):
    lre = lre_ref[...]
    lim = lim_ref[...]
    step = jnp.exp(ls_ref[...])
    for r, k in enumerate(S5_POWERS):
        mag = jnp.exp(k * lre * step)
        ang = k * lim * step
        tab_ref[2 * r:2 * r + 1, :] = mag * jnp.cos(ang)
        tab_ref[2 * r + 1:2 * r + 2, :] = mag * jnp.sin(ang)
    a = tab_ref[0:1, :] - 1.0
    b = tab_ref[1:2, :]
    den = lre * lre + lim * lim
    f_re = (a * lre + b * lim) / den
    f_im = (b * lre - a * lim) / den
    base = 2 * len(S5_POWERS)
    tab_ref[base:base + 1, :] = f_re
    tab_ref[base + 1:base + 2, :] = f_im
    bre = bre_ref[...]
    bim = bim_ref[...]
    bbre_ref[...] = f_re * bre - f_im * bim
    bbim_ref[...] = f_re * bim + f_im * bre


def _s5_prep(lam_re, lam_im, log_step, b_re, b_im):
    lre = lam_re.reshape(1, NS)
    lim = lam_im.reshape(1, NS)
    ls = jnp.repeat(log_step, SSM_STATE).reshape(1, NS)
    bre = jnp.transpose(b_re, (2, 0, 1)).reshape(SSM_GROUP, NS)
    bim = jnp.transpose(b_im, (2, 0, 1)).reshape(SSM_GROUP, NS)
    return pl.pallas_call(
        _s5_prep_kernel,
        out_shape=(jax.ShapeDtypeStruct((S5_ROWS, NS), F32),
                   jax.ShapeDtypeStruct((SSM_GROUP, NS), F32),
                   jax.ShapeDtypeStruct((SSM_GROUP, NS), F32)),
        name="s5_prep",
    )(lre, lim, ls, bre, bim)


def _s5_kernel(u_ref, bm_ref, cm_ref, lk_ref, pw_ref, dsk_ref, wg_ref, bg_ref, o_ref, xs_ref, c_ref, *, lane_chunk):
    tc = pl.program_id(1)

    @pl.when(tc == 0)
    def _():
        c_ref[...] = jnp.zeros(c_ref.shape, F32)

    u = u_ref[...]
    xs_ref[...] = _dot(u.astype(BF16), bm_ref[...])

    n_tiles = S5_L // SUBLANES
    for lc in range(NS // lane_chunk):
        re_cols = slice(lc * lane_chunk, (lc + 1) * lane_chunk)
        im_cols = slice(NS + lc * lane_chunk, NS + (lc + 1) * lane_chunk)
        l_re = [lk_ref[k, 0, :, re_cols] for k in range(3)]
        l_im = [lk_ref[k, 1, :, re_cols] for k in range(3)]
        p_re = pw_ref[0, :, re_cols]
        p_im = pw_ref[1, :, re_cols]

        def tile_body(t, carry, re_cols=re_cols, im_cols=im_cols, l_re=l_re, l_im=l_im, p_re=p_re, p_im=p_im):
            c_re, c_im = carry
            rows = pl.ds(pl.multiple_of(t * SUBLANES, SUBLANES), SUBLANES)
            re = xs_ref[rows, re_cols]
            im = xs_ref[rows, im_cols]
            for k, d in enumerate((1, 2, 4)):
                s_re = pltpu.roll(re, d, 0)
                s_im = pltpu.roll(im, d, 0)
                re, im = (re + l_re[k] * s_re - l_im[k] * s_im,
                          im + l_re[k] * s_im + l_im[k] * s_re)
            re, im = (re + p_re * c_re - p_im * c_im,
                      im + p_re * c_im + p_im * c_re)
            xs_ref[rows, re_cols] = re
            xs_ref[rows, im_cols] = im
            last = SUBLANES - 1
            return (jnp.broadcast_to(re[last:last + 1, :], re.shape),
                    jnp.broadcast_to(im[last:last + 1, :], im.shape))

        c_re, c_im = lax.fori_loop(0, n_tiles, tile_body, (c_ref[:, re_cols], c_ref[:, im_cols]))
        c_ref[:, re_cols] = c_re
        c_ref[:, im_cols] = c_im

    y = _dot(xs_ref[...].astype(BF16), cm_ref[...]) + dsk_ref[...] * u
    y = jax.nn.gelu(y)
    z = _dot(y.astype(BF16), wg_ref[...]) + bg_ref[...]
    o_ref[...] = (y * jax.nn.sigmoid(z)).astype(o_ref.dtype)


def _s5(hm, tab, bb_re, bb_im, c_re, c_im, d_skip, w_glu, b_glu, lane_chunk=512):
    bsz, seq, _ = hm.shape
    eye = jnp.eye(SSM_GROUPS, dtype=F32)

    def in_map(bb):
        bb = jnp.transpose(bb.reshape(SSM_GROUP, SSM_GROUPS, SSM_STATE), (1, 0, 2))
        return jnp.einsum('gpn,gh->gphn', bb, eye).reshape(SSM_WIDTH, NS)

    def out_map(c):
        return jnp.einsum('gpn,gh->gnhp', c, eye).reshape(NS, SSM_WIDTH)

    bm = jnp.concatenate([in_map(bb_re), in_map(bb_im)], axis=1).astype(BF16)
    cm = jnp.concatenate([out_map(c_re), -out_map(c_im)], axis=0).astype(BF16)
    row = jnp.arange(SUBLANES)[:, None]
    lk = jnp.stack([jnp.stack([jnp.where(row >= d, tab[2 * k], 0.0), jnp.where(row >= d, tab[2 * k + 1], 0.0)])
                    for k, d in enumerate((1, 2, 4))])
    pw = jnp.stack([tab[6:6 + 2 * SUBLANES:2], tab[7:7 + 2 * SUBLANES:2]])
    kernel = functools.partial(_s5_kernel, lane_chunk=lane_chunk)
    const2 = lambda b, t: (0, 0)
    return pl.pallas_call(
        kernel,
        grid=(bsz, seq // S5_L),
        in_specs=[
            pl.BlockSpec((None, S5_L, SSM_WIDTH), lambda b, t: (b, t, U_OFF // SSM_WIDTH)),
            pl.BlockSpec(bm.shape, const2),
            pl.BlockSpec(cm.shape, const2),
            pl.BlockSpec(lk.shape, lambda b, t: (0, 0, 0, 0)),
            pl.BlockSpec(pw.shape, lambda b, t: (0, 0, 0)),
            pl.BlockSpec((1, SSM_WIDTH), const2),
            pl.BlockSpec((SSM_WIDTH, SSM_WIDTH), const2),
            pl.BlockSpec((1, SSM_WIDTH), const2),
        ],
        out_specs=pl.BlockSpec((None, S5_L, SSM_WIDTH), lambda b, t: (b, t, 0)),
        out_shape=jax.ShapeDtypeStruct((bsz, seq, SSM_WIDTH), BF16),
        scratch_shapes=[pltpu.VMEM((S5_L, 2 * NS), F32),
                        pltpu.VMEM((SUBLANES, 2 * NS), F32)],
        compiler_params=_cparams(("parallel", "arbitrary")),
        name="s5",
    )(hm, bm, cm, lk, pw, d_skip.reshape(1, -1), w_glu.astype(BF16), b_glu.reshape(1, -1))


def _moba_kernel(q_ref, k_ref, v_ref, bias_ref, o_ref, kb_ref, vb_ref, km_ref, m_ref, l_ref, acc_ref, *, n_blk):
    qb = pl.program_id(2)
    blk = MOBA_BLOCK

    @pl.when(qb == 0)
    def _():
        km_ref[...] = jnp.zeros(km_ref.shape, F32)
        for n in range(n_blk):
            rows = slice(n * blk, (n + 1) * blk)
            kblk = k_ref[rows, :]
            km_ref[n:n + 1, :] = jnp.mean(kblk, axis=0, keepdims=True)
            kb_ref[rows, :] = kblk.astype(BF16)
            vb_ref[rows, :] = v_ref[rows, :].astype(BF16)

    q = q_ref[...]
    lane = lax.broadcasted_iota(I32, (blk, LANES), 1)
    row_i = lax.broadcasted_iota(I32, (blk, blk), 0)
    col_j = lax.broadcasted_iota(I32, (blk, blk), 1)
    km = km_ref[...]
    scale = C_HEAD_DIM ** -0.5

    for s in range(2):
        keep = (lane < C_HEAD_DIM) if s == 0 else (lane >= C_HEAD_DIM)
        qm = jnp.where(keep, q, 0.0)
        gate = lax.dot_general(qm, km, (((1,), (1,)), ((), ())), preferred_element_type=F32,
                               precision=lax.Precision.HIGHEST)
        g = jnp.where(lane < qb, gate, -jnp.inf)
        selm = jnp.zeros((blk, LANES), F32)
        lane_f = lane.astype(F32)
        for _ in range(MOBA_TOPK):
            mx = jnp.max(g, axis=1, keepdims=True)
            is_max = (g == mx) & (g > -jnp.inf)
            first = jnp.min(jnp.where(is_max, lane_f, float(LANES)), axis=1, keepdims=True)
            pick = lane_f == first
            selm = jnp.where(pick, 1.0, selm)
            g = jnp.where(pick, -jnp.inf, g)

        qs = qm.astype(BF16)
        m_ref[s] = jnp.full((blk, 1), NEG_INF, F32)
        l_ref[s] = jnp.zeros((blk, 1), F32)
        acc_ref[s] = jnp.zeros((blk, LANES), F32)

        for n in range(n_blk):
            @pl.when(n <= qb)
            def _(n=n, s=s, qs=qs, selm=selm):
                rows = slice(n * blk, (n + 1) * blk)
                t = jnp.minimum(qb - n, 2)
                sc = _dot_nt(qs, kb_ref[rows, :]) * scale + bias_ref[t, s]
                mask = (row_i - col_j >= (qb - n) * blk) | (selm[:, n:n + 1] > 0.5)
                m_old = m_ref[s]
                m_new = jnp.maximum(m_old, jnp.max(jnp.where(mask, sc, NEG_INF), axis=1, keepdims=True))
                a = jnp.exp(m_old - m_new)
                p = jnp.where(mask, jnp.exp(sc - m_new), 0.0)
                l_ref[s] = a * l_ref[s] + jnp.sum(p, axis=1, keepdims=True)
                acc_ref[s] = a * acc_ref[s] + _dot(p.astype(BF16), vb_ref[rows, :])
                m_ref[s] = m_new

    o = jnp.where(lane < C_HEAD_DIM, acc_ref[0] / l_ref[0], acc_ref[1] / l_ref[1])
    o_ref[...] = o.astype(o_ref.dtype)


def _moba(hm, bias_tiles):
    bsz, seq, _ = hm.shape
    n_blk = seq // MOBA_BLOCK
    n_pair = C_HEADS // 2
    kernel = functools.partial(_moba_kernel, n_blk=n_blk)
    return pl.pallas_call(
        kernel,
        grid=(bsz, n_pair, n_blk),
        in_specs=[
            pl.BlockSpec((None, MOBA_BLOCK, LANES), lambda b, h, q: (b, q, Q_OFF // LANES + h)),
            pl.BlockSpec((None, seq, LANES), lambda b, h, q: (b, 0, K_OFF // LANES + h)),
            pl.BlockSpec((None, seq, LANES), lambda b, h, q: (b, 0, V_OFF // LANES + h)),
            pl.BlockSpec((3, 2, MOBA_BLOCK, MOBA_BLOCK), lambda b, h, q: (0, h, 0, 0)),
        ],
        out_specs=pl.BlockSpec((None, MOBA_BLOCK, LANES), lambda b, h, q: (b, q, h)),
        out_shape=jax.ShapeDtypeStruct((bsz, seq, C_HEADS * C_HEAD_DIM), BF16),
        scratch_shapes=[
            pltpu.VMEM((seq, LANES), BF16),
            pltpu.VMEM((seq, LANES), BF16),
            pltpu.VMEM((LANES, LANES), F32),
            pltpu.VMEM((2, MOBA_BLOCK, 1), F32),
            pltpu.VMEM((2, MOBA_BLOCK, 1), F32),
            pltpu.VMEM((2, MOBA_BLOCK, LANES), F32),
        ],
        compiler_params=_cparams(("parallel", "parallel", "arbitrary")),
        name="moba",
    )(hm, hm, hm, bias_tiles)


def _merge_kernel(x_ref, oa_ref, ob_ref, oc_ref, wg0_ref, wg1_ref, wg2_ref, wb_ref, wo_ref, g_ref, b_ref,
                  o_ref, xb_ref, acc_ref):
    j = pl.program_id(1)

    @pl.when(j == 0)
    def _():
        xb_ref[...] = x_ref[...].astype(BF16)
        acc_ref[...] = jnp.zeros(acc_ref.shape, F32)

    xb = xb_ref[...]
    merged = None
    for n, (o_n, wg_n) in enumerate(((oa_ref, wg0_ref), (ob_ref, wg1_ref), (oc_ref, wg2_ref))):
        gate = jax.nn.sigmoid(_dot(xb, wg_n[...]))
        y = _dot(o_n[...], wb_ref[n])
        merged = gate * y if merged is None else merged + gate * y
    acc_ref[...] += _dot(merged.astype(BF16), wo_ref[...])

    @pl.when(j == pl.num_programs(1) - 1)
    def _():
        z = DEEPNORM_ALPHA * x_ref[...] + acc_ref[...]
        o_ref[...] = _layer_norm(z, g_ref[...], b_ref[...])


def _merge(x2d, oa, ob, oc, wg, wb, wo, ln_g, ln_b, tm=512, tn=256):
    n, d = x2d.shape
    tm = min(tm, n)
    nd = d // tn
    bw = oa.shape[1]
    return pl.pallas_call(
        _merge_kernel,
        grid=(n // tm, nd),
        in_specs=[
            pl.BlockSpec((tm, d), lambda i, j: (i, 0)),
            pl.BlockSpec((tm, bw), lambda i, j: (i, 0)),
            pl.BlockSpec((tm, bw), lambda i, j: (i, 0)),
            pl.BlockSpec((tm, bw), lambda i, j: (i, 0)),
            pl.BlockSpec((d, tn), lambda i, j: (0, j)),
            pl.BlockSpec((d, tn), lambda i, j: (0, nd + j)),
            pl.BlockSpec((d, tn), lambda i, j: (0, 2 * nd + j)),
            pl.BlockSpec((N_BRANCH, bw, tn), lambda i, j: (0, 0, j)),
            pl.BlockSpec((tn, d), lambda i, j: (j, 0)),
            pl.BlockSpec((1, d), lambda i, j: (0, 0)),
            pl.BlockSpec((1, d), lambda i, j: (0, 0)),
        ],
        out_specs=pl.BlockSpec((tm, d), lambda i, j: (i, 0)),
        out_shape=jax.ShapeDtypeStruct((n, d), F32),
        scratch_shapes=[pltpu.VMEM((tm, d), BF16), pltpu.VMEM((tm, d), F32)],
        compiler_params=_cparams(("parallel", "arbitrary")),
        name="merge",
    )(x2d, oa, ob, oc, wg, wg, wg, wb, wo, ln_g.reshape(1, -1), ln_b.reshape(1, -1))


def _ffn_up_kernel(x_ref, xh_ref, wa_ref, wv_ref, cwa_ref, cwv_ref, cba_ref, cbv_ref, o_ref, xb_ref, xhb_ref,
                   *, tiles_per_seq):
    i = pl.program_id(0)
    j = pl.program_id(1)

    @pl.when(j == 0)
    def _():
        xb_ref[...] = x_ref[...].astype(BF16)
        has_prev = (i % tiles_per_seq != 0).astype(F32)
        xhb_ref[...] = (xh_ref[...] * has_prev).astype(BF16)

    xb = xb_ref[...]
    xhb = xhb_ref[...]
    tm, tn = o_ref.shape
    row = lax.broadcasted_iota(I32, (tm, tn), 0)

    def conv(w_ref, cw_ref, cb_ref):
        h = _dot(xb, w_ref[...])
        hh = _dot(xhb, w_ref[...])
        prev1 = jnp.where(row == 0, hh[7:8, :], pltpu.roll(h, 1, 0))
        prev2 = jnp.where(row == 0, hh[6:7, :], jnp.where(row == 1, hh[7:8, :], pltpu.roll(h, 2, 0)))
        return cb_ref[...] + cw_ref[0:1, :] * prev2 + cw_ref[1:2, :] * prev1 + cw_ref[2:3, :] * h

    a = conv(wa_ref, cwa_ref, cba_ref)
    val = conv(wv_ref, cwv_ref, cbv_ref)
    o_ref[...] = (jax.nn.gelu(a) * val).astype(o_ref.dtype)


def _ffn_up(x2d, seq, w_up, conv_w, conv_b, tm=512, tn=512):
    n, d = x2d.shape
    d_ff = w_up.shape[1] // 2
    tm = min(tm, seq)
    nf = d_ff // tn
    kernel = functools.partial(_ffn_up_kernel, tiles_per_seq=seq // tm)
    halo = tm // SUBLANES
    return pl.pallas_call(
        kernel,
        grid=(n // tm, nf),
        in_specs=[
            pl.BlockSpec((tm, d), lambda i, j: (i, 0)),
            pl.BlockSpec((SUBLANES, d), lambda i, j: (jnp.maximum(i * halo - 1, 0), 0)),
            pl.BlockSpec((d, tn), lambda i, j: (0, j)),
            pl.BlockSpec((d, tn), lambda i, j: (0, nf + j)),
            pl.BlockSpec((CONV_WIDTH, tn), lambda i, j: (0, j)),
            pl.BlockSpec((CONV_WIDTH, tn), lambda i, j: (0, nf + j)),
            pl.BlockSpec((1, tn), lambda i, j: (0, j)),
            pl.BlockSpec((1, tn), lambda i, j: (0, nf + j)),
        ],
        out_specs=pl.BlockSpec((tm, tn), lambda i, j: (i, j)),
        out_shape=jax.ShapeDtypeStruct((n, d_ff), BF16),
        scratch_shapes=[pltpu.VMEM((tm, d), BF16), pltpu.VMEM((SUBLANES, d), BF16)],
        compiler_params=_cparams(("parallel", "arbitrary")),
        name="ffn_up",
    )(x2d, x2d, w_up, w_up, conv_w, conv_w, conv_b.reshape(1, -1), conv_b.reshape(1, -1))


def _ffn_down_kernel(a_ref, w_ref, x_ref, g_ref, b_ref, o_ref, acc_ref):
    k = pl.program_id(1)

    @pl.when(k == 0)
    def _():
        acc_ref[...] = jnp.zeros(acc_ref.shape, F32)

    acc_ref[...] += _dot(a_ref[...], w_ref[...])

    @pl.when(k == pl.num_programs(1) - 1)
    def _():
        z = DEEPNORM_ALPHA * x_ref[...] + acc_ref[...]
        o_ref[...] = _layer_norm(z, g_ref[...], b_ref[...])


def _ffn_down(act, w_down, x2d, ln_g, ln_b, tm=512, tk=512):
    n, d = x2d.shape
    d_ff = act.shape[1]
    tm = min(tm, n)
    return pl.pallas_call(
        _ffn_down_kernel,
        grid=(n // tm, d_ff // tk),
        in_specs=[
            pl.BlockSpec((tm, tk), lambda i, k: (i, k)),
            pl.BlockSpec((tk, d), lambda i, k: (k, 0)),
            pl.BlockSpec((tm, d), lambda i, k: (i, 0)),
            pl.BlockSpec((1, d), lambda i, k: (0, 0)),
            pl.BlockSpec((1, d), lambda i, k: (0, 0)),
        ],
        out_specs=pl.BlockSpec((tm, d), lambda i, k: (i, 0)),
        out_shape=jax.ShapeDtypeStruct((n, d), F32),
        scratch_shapes=[pltpu.VMEM((tm, d), F32)],
        compiler_params=_cparams(("parallel", "arbitrary")),
        name="ffn_down",
    )(act, w_down, x2d, ln_g.reshape(1, -1), ln_b.reshape(1, -1))


def _mixer_weight(w_in_l):
    d = w_in_l.shape[0]
    sizes = (A_Q_RANK, A_KV_RANK, IDX_DIM, IDX_HEADS, SSM_WIDTH, 512, 512, 512)
    offs = np.cumsum((0,) + sizes)
    seg = [w_in_l[:, offs[i]:offs[i + 1]] for i in range(len(sizes))]
    pad = lambda w: jnp.zeros((d, w), w_in_l.dtype)
    cols = [seg[0], pad(CKV_OFF - A_Q_RANK), seg[1], seg[2], seg[3], pad(U_OFF - KW_OFF - IDX_DIM - IDX_HEADS),
            seg[4], seg[5], seg[6], seg[7]]
    w = jnp.concatenate(cols, axis=1)
    assert w.shape[1] == PROJ_W
    return w.astype(BF16), w_in_l[:, offs[-1]:].astype(BF16)


def kernel(x, rel_bias, w_in, cq_gain, ckv_gain, w_uq, w_uk, w_uv, w_qidx, lam_re, lam_im, log_step, b_re, b_im, c_re, c_im, d_skip, w_glu, b_glu, w_branch, w_out, ln1_g, ln1_b, w_up, conv_w, conv_b, w_down, ln2_g, ln2_b):
    bsz, seq, d = x.shape
    assert seq % MOBA_BLOCK == 0 and seq % DSA_KC == 0 and seq % S5_L == 0
    n = bsz * seq

    tab_t = jnp.transpose(rel_bias)
    dsa_bkt = jnp.asarray(_toeplitz_buckets(DSA_QB, DSA_KC, (0, DSA_QB, 2 * DSA_QB)))
    moba_bkt = jnp.asarray(_toeplitz_buckets(MOBA_BLOCK, MOBA_BLOCK, (0, MOBA_BLOCK)))
    dsa_bias = _bias_expand(dsa_bkt, tab_t[:A_HEADS]).reshape(A_HEADS, 4, DSA_QB, DSA_KC)
    dsa_bias = jnp.transpose(dsa_bias, (1, 0, 2, 3))
    moba_bias = _bias_expand(moba_bkt, tab_t[A_HEADS:]).reshape(C_HEADS, 3, MOBA_BLOCK, MOBA_BLOCK)
    moba_bias = jnp.transpose(moba_bias, (1, 0, 2, 3))

    x2d = x.reshape(n, d)
    for l in range(DEPTH):
        w_mix, w_gate = _mixer_weight(w_in[l])
        hm = _proj(x2d, w_mix).reshape(bsz, seq, PROJ_W)
        o_a = _dsa(hm, cq_gain[l], ckv_gain[l], w_uq[l], w_uk[l], w_uv[l], w_qidx[l], dsa_bias)
        tab, bb_re, bb_im = _s5_prep(lam_re[l], lam_im[l], log_step[l], b_re[l], b_im[l])
        o_b = _s5(hm, tab, bb_re, bb_im, c_re[l], c_im[l], d_skip[l], w_glu[l], b_glu[l])
        o_c = _moba(hm, moba_bias)
        bw = o_a.shape[-1]
        x2d = _merge(x2d, o_a.reshape(n, bw), o_b.reshape(n, bw), o_c.reshape(n, bw), w_gate,
                     w_branch[l].astype(BF16), w_out[l].astype(BF16), ln1_g[l], ln1_b[l])
        act = _ffn_up(x2d, seq, w_up[l].astype(BF16), conv_w[l], conv_b[l])
        x2d = _ffn_down(act, w_down[l].astype(BF16), x2d, ln2_g[l], ln2_b[l])
    return x2d.reshape(bsz, seq, d)
```

```python
import functools
import math

import numpy as np
import jax
import jax.numpy as jnp
from jax import lax
from jax.experimental import pallas as pl
from jax.experimental.pallas import tpu as pltpu

F32 = jnp.float32
BF16 = jnp.bfloat16
I32 = jnp.int32

DEPTH = 2
A_HEADS = 8
A_HEAD_DIM = 64
A_Q_RANK = 384
A_KV_RANK = 256
IDX_HEADS = 16
IDX_DIM = 64
DSA_TOPK = 256
SSM_GROUP = 16
SSM_GROUPS = 32
SSM_WIDTH = SSM_GROUP * SSM_GROUPS
SSM_STATE = 64
C_HEADS = 8
C_HEAD_DIM = 64
MOBA_BLOCK = 256
MOBA_TOPK = 3
N_BRANCH = 3
BRANCH_WIDTH = 512
CONV_WIDTH = 3
REL_BUCKETS = 32
REL_MAX_DIST = 128
LN_EPS = 1e-5
NEG_INF = -1e30
DEEPNORM_ALPHA = (2 * DEPTH) ** 0.25

LANES = 128
SUBLANES = 8
VMEM_LIMIT = 56 * 1024 * 1024

CQ_OFF, CKV_OFF, KW_OFF, U_OFF, Q_OFF, K_OFF, V_OFF, PROJ_W = 0, 512, 768, 1024, 1536, 2048, 2560, 3072

DSA_QB = 128
DSA_KC = 256
S5_L = 256
NS = SSM_GROUPS * SSM_STATE


def _cparams(sem):
    return pltpu.CompilerParams(dimension_semantics=sem, vmem_limit_bytes=VMEM_LIMIT)


def _dot(a, b):
    return jnp.dot(a, b, preferred_element_type=F32)


def _dot_nt(a, b):
    return lax.dot_general(a, b, (((1,), (1,)), ((), ())), preferred_element_type=F32)


def _order_key(s):
    bits = lax.bitcast_convert_type(s, I32)
    return bits ^ ((bits >> 31) & 0x7FFFFFFF)


def _layer_norm(z, g, b):
    mu = jnp.mean(z, axis=-1, keepdims=True)
    zc = z - mu
    var = jnp.mean(zc * zc, axis=-1, keepdims=True)
    return zc * lax.rsqrt(var + LN_EPS) * g + b


def _proj_kernel(x_ref, w_ref, o_ref, xb_ref):
    @pl.when(pl.program_id(1) == 0)
    def _():
        xb_ref[...] = x_ref[...].astype(BF16)

    o_ref[...] = _dot(xb_ref[...], w_ref[...])


def _proj(x2d, w, tm=1024, tn=512):
    n, d = x2d.shape
    pw = w.shape[1]
    tm = min(tm, n)
    return pl.pallas_call(
        _proj_kernel,
        grid=(n // tm, pw // tn),
        in_specs=[pl.BlockSpec((tm, d), lambda i, j: (i, 0)),
                  pl.BlockSpec((d, tn), lambda i, j: (0, j))],
        out_specs=pl.BlockSpec((tm, tn), lambda i, j: (i, j)),
        out_shape=jax.ShapeDtypeStruct((n, pw), F32),
        scratch_shapes=[pltpu.VMEM((tm, d), BF16)],
        compiler_params=_cparams(("parallel", "arbitrary")),
        name="proj",
    )(x2d, w)


def _bias_kernel(bkt_ref, tab_ref, o_ref):
    b = bkt_ref[...]
    acc = jnp.zeros(o_ref.shape, F32)
    for k in range(REL_BUCKETS):
        acc = jnp.where(b == k, tab_ref[:, k:k + 1], acc)
    o_ref[...] = acc


def _bias_expand(bucket, tab_t, pc=8192):
    p = bucket.shape[1]
    h = tab_t.shape[0]
    return pl.pallas_call(
        _bias_kernel,
        grid=(p // pc,),
        in_specs=[pl.BlockSpec((1, pc), lambda i: (0, i)),
                  pl.BlockSpec((h, REL_BUCKETS), lambda i: (0, 0))],
        out_specs=pl.BlockSpec((h, pc), lambda i: (0, i)),
        out_shape=jax.ShapeDtypeStruct((h, p), F32),
        compiler_params=_cparams(("parallel",)),
        name="bias_expand",
    )(bucket, tab_t)


def _t5_bucket_np(dist):
    n = np.maximum(dist, 0)
    exact = REL_BUCKETS // 2
    log_ratio = np.log(np.maximum(n, 1).astype(np.float32) / exact) / math.log(REL_MAX_DIST / exact)
    large = np.minimum(exact + (log_ratio * (REL_BUCKETS - exact)).astype(np.int32), REL_BUCKETS - 1)
    return np.where(n < exact, n, large).astype(np.int32)


def _toeplitz_buckets(rows, cols, deltas):
    i = np.arange(rows)[:, None]
    j = np.arange(cols)[None, :]
    tiles = [_t5_bucket_np(d + i - j) for d in deltas]
    tiles.append(np.full((rows, cols), REL_BUCKETS - 1, np.int32))
    return np.stack(tiles).reshape(1, -1)


def _dsa_kernel(cq_ref, ckv_ref, kw_ref, cqg_ref, ckvg_ref, wuq_ref, wuk_ref, wqi_ref, wuv_ref, bias_ref,
                o_ref, kvn_ref, kk_ref, key_ref, key_t_ref, ql_ref, lg_ref, mx_ref, ls_ref, p_ref, acc_ref, *, seq, n_top):
    qb = pl.program_id(1)
    q0 = qb * DSA_QB
    n_chunk_rows = seq // DSA_KC

    @pl.when(qb == 0)
    def _():
        lane = lax.broadcasted_iota(I32, (DSA_KC, LANES), 1)

        def prep(c, carry):
            rows = pl.ds(pl.multiple_of(c * DSA_KC, DSA_KC), DSA_KC)
            ckv = ckv_ref[rows, :]
            ms = jnp.mean(ckv * ckv, axis=-1, keepdims=True)
            kvn_ref[rows, :] = (ckv * lax.rsqrt(ms + LN_EPS) * ckvg_ref[...]).astype(BF16)
            kw = kw_ref[rows, :]
            kk_ref[rows, :] = jnp.where(lane < IDX_DIM, kw, pltpu.roll(kw, IDX_DIM, 1)).astype(BF16)
            return carry

        lax.fori_loop(0, n_chunk_rows, prep, 0)

    nc = (q0 + DSA_QB + DSA_KC - 1) // DSA_KC

    cq = cq_ref[...]
    ms = jnp.mean(cq * cq, axis=-1, keepdims=True)
    cqn = (cq * lax.rsqrt(ms + LN_EPS) * cqg_ref[...]).astype(BF16)

    lane_q = lax.broadcasted_iota(I32, (DSA_QB, LANES), 1)
    lo_half = lane_q < IDX_DIM
    row_i = lax.broadcasted_iota(I32, (DSA_QB, DSA_KC), 0)
    col_j = lax.broadcasted_iota(I32, (DSA_QB, DSA_KC), 1)
    qpos = q0 + row_i

    qi = _dot(cqn, wqi_ref[...]) * (IDX_DIM ** -0.5)
    wq = kw_ref[pl.ds(pl.multiple_of(q0, DSA_QB), DSA_QB), :] * (IDX_HEADS ** -0.5)
    q_heads = []
    for h in range(IDX_HEADS):
        qp = qi[:, (h // 2) * LANES:(h // 2 + 1) * LANES]
        keep = lo_half if h % 2 == 0 else jnp.logical_not(lo_half)
        q_heads.append(jnp.where(keep, qp, 0.0).astype(BF16))

    def score_body(c, carry):
        kc = kk_ref[pl.ds(pl.multiple_of(c * DSA_KC, DSA_KC), DSA_KC), :]
        s = jnp.zeros((DSA_QB, DSA_KC), F32)
        for h in range(IDX_HEADS):
            r = _dot_nt(q_heads[h], kc)
            s = s + wq[:, IDX_DIM + h:IDX_DIM + h + 1] * jnp.maximum(r, 0.0)
        s = jnp.where(qpos >= c * DSA_KC + col_j, s, -jnp.inf)
        key_ref[c] = _order_key(s)
        key_t_ref[c] = _order_key(s.T).reshape(DSA_KC // SUBLANES, SUBLANES, DSA_QB)
        return carry

    lax.fori_loop(0, nc, score_body, 0)

    n_grp = DSA_KC // SUBLANES
    kpos_t = (lax.broadcasted_iota(I32, (n_grp, SUBLANES, DSA_QB), 0) * SUBLANES
              + lax.broadcasted_iota(I32, (n_grp, SUBLANES, DSA_QB), 1))

    def count(pred):
        def body(c, tot):
            one = jnp.where(pred(key_t_ref[c], c), 1.0, 0.0)
            parts = [one[g] for g in range(n_grp)]
            while len(parts) > 1:
                parts = [parts[i] + parts[i + 1] for i in range(0, len(parts), 2)]
            return tot + parts[0]

        tot = lax.fori_loop(0, nc, body, jnp.zeros((SUBLANES, DSA_QB), F32))
        for d in (4, 2, 1):
            tot = tot + pltpu.roll(tot, d, 0)
        return tot

    int_min = jnp.int32(-2 ** 31)
    k_top = float(n_top)
    cnt = count(lambda k, c: k >= 0)
    thr0 = jnp.where(cnt >= k_top, jnp.int32(0), int_min)

    def bit_body(i, thr):
        cand = thr | jnp.left_shift(jnp.int32(1), 30 - i)
        cnt = count(lambda k, c: k >= cand)
        return jnp.where(cnt >= k_top, cand, thr)

    thr_t = lax.fori_loop(0, 31, bit_body, thr0)

    need = k_top - count(lambda k, c: k > thr_t)
    n_eq = count(lambda k, c: k == thr_t)
    any_tie = jnp.max(jnp.where(n_eq > need, 1.0, 0.0)) > 0.0
    idx_bits = int(math.log2(seq))

    def tie_cut():
        def body(i, v):
            cand = v | jnp.left_shift(jnp.int32(1), idx_bits - 1 - i)
            cnt = count(lambda k, c: (k == thr_t) & (c * DSA_KC + kpos_t < cand))
            return jnp.where(cnt < need, cand, v)

        return lax.fori_loop(0, idx_bits, body, jnp.zeros((SUBLANES, DSA_QB), I32))

    cut_t = lax.cond(any_tie, tie_cut, lambda: jnp.full((SUBLANES, DSA_QB), seq, I32))

    def to_rows(v_t):
        full = jnp.broadcast_to(v_t[0:1, :], (DSA_QB, DSA_QB))
        col = lax.bitcast_convert_type(lax.bitcast_convert_type(full, F32).T, I32)
        return jnp.concatenate([col] * (DSA_KC // DSA_QB), axis=1)

    thr = to_rows(thr_t)
    cut = to_rows(cut_t)

    q = _dot(cqn, wuq_ref[...])
    for h in range(A_HEADS):
        qp = q[:, (h // 2) * LANES:(h // 2 + 1) * LANES]
        keep = lo_half if h % 2 == 0 else jnp.logical_not(lo_half)
        qm = jnp.where(keep, qp, 0.0).astype(BF16)
        ql_ref[h * DSA_QB:(h + 1) * DSA_QB, :] = (_dot(qm, wuk_ref[h // 2]) * (A_HEAD_DIM ** -0.5)).astype(BF16)
    mx_ref[...] = jnp.full(mx_ref.shape, NEG_INF, F32)
    ls_ref[...] = jnp.zeros(ls_ref.shape, F32)
    acc_ref[...] = jnp.zeros(acc_ref.shape, F32)

    def logit_body(c, carry):
        kv = kvn_ref[pl.ds(pl.multiple_of(c * DSA_KC, DSA_KC), DSA_KC), :]
        k = key_ref[c]
        kpos = c * DSA_KC + col_j
        sel = ((k > thr) | ((k == thr) & (kpos <= cut))) & (qpos >= kpos)
        add_mask = jnp.where(sel, 0.0, NEG_INF)
        t = jnp.minimum((q0 - c * DSA_KC) // DSA_QB, 3)
        s_all = _dot_nt(ql_ref[...], kv)
        for h in range(A_HEADS):
            rows = slice(h * DSA_QB, (h + 1) * DSA_QB)
            sm = s_all[rows, :] + bias_ref[t, h] + add_mask
            lg_ref[c, rows, :] = sm
            mx_ref[h] = jnp.maximum(mx_ref[h], jnp.maximum(sm[:, :LANES], sm[:, LANES:]))
        return carry

    lax.fori_loop(0, nc, logit_body, 0)

    for h in range(A_HEADS):
        mx_ref[h] = jnp.broadcast_to(jnp.max(mx_ref[h], axis=1, keepdims=True), (DSA_QB, LANES))

    def pv_body(c, carry):
        kv = kvn_ref[pl.ds(pl.multiple_of(c * DSA_KC, DSA_KC), DSA_KC), :]
        for h in range(A_HEADS):
            rows = slice(h * DSA_QB, (h + 1) * DSA_QB)
            m = mx_ref[h]
            p_lo = jnp.exp(lg_ref[c, rows, :LANES] - m)
            p_hi = jnp.exp(lg_ref[c, rows, LANES:] - m)
            ls_ref[h] += p_lo + p_hi
            p_ref[rows, :LANES] = p_lo.astype(BF16)
            p_ref[rows, LANES:] = p_hi.astype(BF16)
        acc_ref[...] += _dot(p_ref[...], kv)
        return carry

    lax.fori_loop(0, nc, pv_body, 0)

    for pr in range(A_HEADS // 2):
        o = jnp.zeros((DSA_QB, LANES), F32)
        for h in (2 * pr, 2 * pr + 1):
            l = jnp.sum(ls_ref[h], axis=1, keepdims=True)
            o_lat = (acc_ref[h * DSA_QB:(h + 1) * DSA_QB, :] / l).astype(BF16)
            o = o + _dot(o_lat, wuv_ref[h])
        o_ref[:, pr * LANES:(pr + 1) * LANES] = o.astype(o_ref.dtype)


def _dsa(hm, cq_gain, ckv_gain, w_uq, w_uk, w_uv, w_qidx, bias_tiles):
    bsz, seq, _ = hm.shape
    n_top = min(DSA_TOPK, seq // 4)
    hd = A_HEADS * A_HEAD_DIM
    wuq = w_uq.reshape(A_Q_RANK, hd).astype(BF16)
    wuk = jnp.transpose(w_uk, (1, 2, 0)).reshape(A_HEADS // 2, 2 * A_HEAD_DIM, A_KV_RANK).astype(BF16)
    wqi = w_qidx.reshape(A_Q_RANK, IDX_HEADS * IDX_DIM).astype(BF16)
    wv = jnp.transpose(w_uv, (1, 0, 2))
    zeros = jnp.zeros_like(wv)
    even = (jnp.arange(A_HEADS) % 2 == 0)[:, None, None]
    wuv = jnp.concatenate([jnp.where(even, wv, zeros), jnp.where(even, zeros, wv)], axis=-1).astype(BF16)
    kernel = functools.partial(_dsa_kernel, seq=seq, n_top=n_top)
    const2 = lambda b, q: (0, 0)
    const3 = lambda b, q: (0, 0, 0)
    return pl.pallas_call(
        kernel,
        grid=(bsz, seq // DSA_QB),
        in_specs=[
            pl.BlockSpec((None, DSA_QB, A_Q_RANK), lambda b, q: (b, q, CQ_OFF // A_Q_RANK)),
            pl.BlockSpec((None, seq, A_KV_RANK), lambda b, q: (b, 0, CKV_OFF // A_KV_RANK)),
            pl.BlockSpec((None, seq, LANES), lambda b, q: (b, 0, KW_OFF // LANES)),
            pl.BlockSpec((1, A_Q_RANK), const2),
            pl.BlockSpec((1, A_KV_RANK), const2),
            pl.BlockSpec(wuq.shape, const2),
            pl.BlockSpec(wuk.shape, const3),
            pl.BlockSpec(wqi.shape, const2),
            pl.BlockSpec(wuv.shape, const3),
            pl.BlockSpec(bias_tiles.shape, lambda b, q: (0, 0, 0, 0)),
        ],
        out_specs=pl.BlockSpec((None, DSA_QB, hd), lambda b, q: (b, q, 0)),
        out_shape=jax.ShapeDtypeStruct((bsz, seq, hd), BF16),
        scratch_shapes=[
            pltpu.VMEM((seq, A_KV_RANK), BF16),
            pltpu.VMEM((seq, LANES), BF16),
            pltpu.VMEM((seq // DSA_KC, DSA_QB, DSA_KC), I32),
            pltpu.VMEM((seq // DSA_KC, DSA_KC // SUBLANES, SUBLANES, DSA_QB), I32),
            pltpu.VMEM((A_HEADS * DSA_QB, A_KV_RANK), BF16),
            pltpu.VMEM((seq // DSA_KC, A_HEADS * DSA_QB, DSA_KC), F32),
            pltpu.VMEM((A_HEADS, DSA_QB, LANES), F32),
            pltpu.VMEM((A_HEADS, DSA_QB, LANES), F32),
            pltpu.VMEM((A_HEADS * DSA_QB, DSA_KC), BF16),
            pltpu.VMEM((A_HEADS * DSA_QB, A_KV_RANK), F32),
        ],
        compiler_params=_cparams(("parallel", "arbitrary")),
        name="dsa",
    )(hm, hm, hm, cq_gain.reshape(1, -1), ckv_gain.reshape(1, -1), wuq, wuk, wqi, wuv, bias_tiles)


S5_POWERS = (1, 2, 4) + tuple(range(1, SUBLANES + 1))
S5_ROWS = 2 * len(S5_POWERS) + 2


def _s5_prep_kernel(lre_ref, lim_ref, ls_ref, bre_ref, bim_ref, tab_ref, bbre_ref, bbim_ref):
    lre = lre_ref[...]
    lim = lim_ref[...]
    step = jnp.exp(ls_ref[...])
    for r, k in enumerate(S5_POWERS):
        mag = jnp.exp(k * lre * step)
        ang = k * lim * step
        tab_ref[2 * r:2 * r + 1, :] = mag * jnp.cos(ang)
        tab_ref[2 * r + 1:2 * r + 2, :] = mag * jnp.sin(ang)
    a = tab_ref[0:1, :] - 1.0
    b = tab_ref[1:2, :]
    den = lre * lre + lim * lim
    f_re = (a * lre + b * lim) / den
    f_im = (b * lre - a * lim) / den
    base = 2 * len(S5_POWERS)
    tab_ref[base:base + 1, :] = f_re
    tab_ref[base + 1:base + 2, :] = f_im
    bre = bre_ref[...]
    bim = bim_ref[...]
    bbre_ref[...] = f_re * bre - f_im * bim
    bbim_ref[...] = f_re * bim + f_im * bre


def _s5_prep(lam_re, lam_im, log_step, b_re, b_im):
    lre = lam_re.reshape(1, NS)
    lim = lam_im.reshape(1, NS)
    ls = jnp.repeat(log_step, SSM_STATE).reshape(1, NS)
    bre = jnp.transpose(b_re, (2, 0, 1)).reshape(SSM_GROUP, NS)
    bim = jnp.transpose(b_im, (2, 0, 1)).reshape(SSM_GROUP, NS)
    return pl.pallas_call(
        _s5_prep_kernel,
        out_shape=(jax.ShapeDtypeStruct((S5_ROWS, NS), F32),
                   jax.ShapeDtypeStruct((SSM_GROUP, NS), F32),
                   jax.ShapeDtypeStruct((SSM_GROUP, NS), F32)),
        name="s5_prep",
    )(lre, lim, ls, bre, bim)


def _s5_kernel(u_ref, bm_ref, cm_ref, lk_ref, pw_ref, dsk_ref, wg_ref, bg_ref, o_ref, xs_ref, c_ref, *, lane_chunk):
    tc = pl.program_id(1)

    @pl.when(tc == 0)
    def _():
        c_ref[...] = jnp.zeros(c_ref.shape, F32)

    u = u_ref[...]
    xs_ref[...] = _dot(u.astype(BF16), bm_ref[...])

    n_tiles = S5_L // SUBLANES
    for lc in range(NS // lane_chunk):
        re_cols = slice(lc * lane_chunk, (lc + 1) * lane_chunk)
        im_cols = slice(NS + lc * lane_chunk, NS + (lc + 1) * lane_chunk)
        l_re = [lk_ref[k, 0, :, re_cols] for k in range(3)]
        l_im = [lk_ref[k, 1, :, re_cols] for k in range(3)]
        p_re = pw_ref[0, :, re_cols]
        p_im = pw_ref[1, :, re_cols]

        def tile_body(t, carry, re_cols=re_cols, im_cols=im_cols, l_re=l_re, l_im=l_im, p_re=p_re, p_im=p_im):
            c_re, c_im = carry
            rows = pl.ds(pl.multiple_of(t * SUBLANES, SUBLANES), SUBLANES)
            re = xs_ref[rows, re_cols]
            im = xs_ref[rows, im_cols]
            for k, d in enumerate((1, 2, 4)):
                s_re = pltpu.roll(re, d, 0)
                s_im = pltpu.roll(im, d, 0)
                re, im = (re + l_re[k] * s_re - l_im[k] * s_im,
                          im + l_re[k] * s_im + l_im[k] * s_re)
            re, im = (re + p_re * c_re - p_im * c_im,
                      im + p_re * c_im + p_im * c_re)
            xs_ref[rows, re_cols] = re
            xs_ref[rows, im_cols] = im
            last = SUBLANES - 1
            return (jnp.broadcast_to(re[last:last + 1, :], re.shape),
                    jnp.broadcast_to(im[last:last + 1, :], im.shape))

        c_re, c_im = lax.fori_loop(0, n_tiles, tile_body, (c_ref[:, re_cols], c_ref[:, im_cols]))
        c_ref[:, re_cols] = c_re
        c_ref[:, im_cols] = c_im

    y = _dot(xs_ref[...].astype(BF16), cm_ref[...]) + dsk_ref[...] * u
    y = jax.nn.gelu(y)
    z = _dot(y.astype(BF16), wg_ref[...]) + bg_ref[...]
    o_ref[...] = (y * jax.nn.sigmoid(z)).astype(o_ref.dtype)


def _s5(hm, tab, bb_re, bb_im, c_re, c_im, d_skip, w_glu, b_glu, lane_chunk=512):
    bsz, seq, _ = hm.shape
    eye = jnp.eye(SSM_GROUPS, dtype=F32)

    def in_map(bb):
        bb = jnp.transpose(bb.reshape(SSM_GROUP, SSM_GROUPS, SSM_STATE), (1, 0, 2))
        return jnp.einsum('gpn,gh->gphn', bb, eye).reshape(SSM_WIDTH, NS)

    def out_map(c):
        return jnp.einsum('gpn,gh->gnhp', c, eye).reshape(NS, SSM_WIDTH)

    bm = jnp.concatenate([in_map(bb_re), in_map(bb_im)], axis=1).astype(BF16)
    cm = jnp.concatenate([out_map(c_re), -out_map(c_im)], axis=0).astype(BF16)
    row = jnp.arange(SUBLANES)[:, None]
    lk = jnp.stack([jnp.stack([jnp.where(row >= d, tab[2 * k], 0.0), jnp.where(row >= d, tab[2 * k + 1], 0.0)])
                    for k, d in enumerate((1, 2, 4))])
    pw = jnp.stack([tab[6:6 + 2 * SUBLANES:2], tab[7:7 + 2 * SUBLANES:2]])
    kernel = functools.partial(_s5_kernel, lane_chunk=lane_chunk)
    const2 = lambda b, t: (0, 0)
    return pl.pallas_call(
        kernel,
        grid=(bsz, seq // S5_L),
        in_specs=[
            pl.BlockSpec((None, S5_L, SSM_WIDTH), lambda b, t: (b, t, U_OFF // SSM_WIDTH)),
            pl.BlockSpec(bm.shape, const2),
            pl.BlockSpec(cm.shape, const2),
            pl.BlockSpec(lk.shape, lambda b, t: (0, 0, 0, 0)),
            pl.BlockSpec(pw.shape, lambda b, t: (0, 0, 0)),
            pl.BlockSpec((1, SSM_WIDTH), const2),
            pl.BlockSpec((SSM_WIDTH, SSM_WIDTH), const2),
            pl.BlockSpec((1, SSM_WIDTH), const2),
        ],
        out_specs=pl.BlockSpec((None, S5_L, SSM_WIDTH), lambda b, t: (b, t, 0)),
        out_shape=jax.ShapeDtypeStruct((bsz, seq, SSM_WIDTH), BF16),
        scratch_shapes=[pltpu.VMEM((S5_L, 2 * NS), F32),
                        pltpu.VMEM((SUBLANES, 2 * NS), F32)],
        compiler_params=_cparams(("parallel", "arbitrary")),
        name="s5",
    )(hm, bm, cm, lk, pw, d_skip.reshape(1, -1), w_glu.astype(BF16), b_glu.reshape(1, -1))


def _moba_kernel(q_ref, k_ref, v_ref, bias_ref, o_ref, kb_ref, vb_ref, km_ref, lg_ref, mx_ref, ls_ref, p_ref, acc_ref,
                 *, n_blk):
    qb = pl.program_id(2)
    blk = MOBA_BLOCK

    @pl.when(qb == 0)
    def _():
        km_ref[...] = jnp.zeros(km_ref.shape, F32)
        for n in range(n_blk):
            rows = slice(n * blk, (n + 1) * blk)
            kblk = k_ref[rows, :]
            km_ref[n:n + 1, :] = jnp.mean(kblk, axis=0, keepdims=True)
            kb_ref[rows, :] = kblk.astype(BF16)
            vb_ref[rows, :] = v_ref[rows, :].astype(BF16)

    q = q_ref[...]
    lane = lax.broadcasted_iota(I32, (blk, LANES), 1)
    lane_f = lane.astype(F32)
    row_i = lax.broadcasted_iota(I32, (blk, blk), 0)
    col_j = lax.broadcasted_iota(I32, (blk, blk), 1)
    causal_add = jnp.where(row_i >= col_j, 0.0, NEG_INF)
    km = km_ref[...]
    scale = C_HEAD_DIM ** -0.5
    k_own = kb_ref[pl.ds(pl.multiple_of(qb * blk, blk), blk), :]

    qs, selm = [], []
    for s in range(2):
        keep = (lane < C_HEAD_DIM) if s == 0 else (lane >= C_HEAD_DIM)
        qm = jnp.where(keep, q, 0.0)
        gate = lax.dot_general(qm, km, (((1,), (1,)), ((), ())), preferred_element_type=F32,
                               precision=lax.Precision.HIGHEST)
        g = jnp.where(lane < qb, gate, -jnp.inf)
        sel = jnp.zeros((blk, LANES), F32)
        for _ in range(MOBA_TOPK):
            mx = jnp.max(g, axis=1, keepdims=True)
            is_max = (g == mx) & (g > -jnp.inf)
            first = jnp.min(jnp.where(is_max, lane_f, float(LANES)), axis=1, keepdims=True)
            pick = lane_f == first
            sel = jnp.where(pick, 1.0, sel)
            g = jnp.where(pick, -jnp.inf, g)
        selm.append(sel)
        qs.append((qm * scale).astype(BF16))
        sm = _dot_nt(qs[s], k_own) + bias_ref[0, s] + causal_add
        lg_ref[s, qb] = sm
        mx_ref[s] = jnp.maximum(sm[:, :LANES], sm[:, LANES:])
        ls_ref[s] = jnp.zeros((blk, LANES), F32)
        acc_ref[s] = jnp.zeros((blk, LANES), F32)

    def logit_body(n, carry):
        kc = kb_ref[pl.ds(pl.multiple_of(n * blk, blk), blk), :]
        t = jnp.minimum(qb - n, 2)
        for s in range(2):
            picked = jnp.max(jnp.where(lane == n, selm[s], 0.0), axis=1, keepdims=True)
            sm = _dot_nt(qs[s], kc) + bias_ref[t, s] + jnp.where(picked > 0.5, 0.0, NEG_INF)
            lg_ref[s, n] = sm
            mx_ref[s] = jnp.maximum(mx_ref[s], jnp.maximum(sm[:, :LANES], sm[:, LANES:]))
        return carry

    lax.fori_loop(0, qb, logit_body, 0)

    for s in range(2):
        mx_ref[s] = jnp.broadcast_to(jnp.max(mx_ref[s], axis=1, keepdims=True), (blk, LANES))

    def pv_body(n, carry):
        vc = vb_ref[pl.ds(pl.multiple_of(n * blk, blk), blk), :]
        for s in range(2):
            m = mx_ref[s]
            p_lo = jnp.exp(lg_ref[s, n, :, :LANES] - m)
            p_hi = jnp.exp(lg_ref[s, n, :, LANES:] - m)
            ls_ref[s] += p_lo + p_hi
            p_ref[s, :, :LANES] = p_lo.astype(BF16)
            p_ref[s, :, LANES:] = p_hi.astype(BF16)
            acc_ref[s] += _dot(p_ref[s], vc)
        return carry

    lax.fori_loop(0, qb + 1, pv_body, 0)

    o = jnp.where(lane < C_HEAD_DIM,
                  acc_ref[0] / jnp.sum(ls_ref[0], axis=1, keepdims=True),
                  acc_ref[1] / jnp.sum(ls_ref[1], axis=1, keepdims=True))
    o_ref[...] = o.astype(o_ref.dtype)


def _moba(hm, bias_tiles):
    bsz, seq, _ = hm.shape
    n_blk = seq // MOBA_BLOCK
    n_pair = C_HEADS // 2
    kernel = functools.partial(_moba_kernel, n_blk=n_blk)
    return pl.pallas_call(
        kernel,
        grid=(bsz, n_pair, n_blk),
        in_specs=[
            pl.BlockSpec((None, MOBA_BLOCK, LANES), lambda b, h, q: (b, q, Q_OFF // LANES + h)),
            pl.BlockSpec((None, seq, LANES), lambda b, h, q: (b, 0, K_OFF // LANES + h)),
            pl.BlockSpec((None, seq, LANES), lambda b, h, q: (b, 0, V_OFF // LANES + h)),
            pl.BlockSpec((3, 2, MOBA_BLOCK, MOBA_BLOCK), lambda b, h, q: (0, h, 0, 0)),
        ],
        out_specs=pl.BlockSpec((None, MOBA_BLOCK, LANES), lambda b, h, q: (b, q, h)),
        out_shape=jax.ShapeDtypeStruct((bsz, seq, C_HEADS * C_HEAD_DIM), BF16),
        scratch_shapes=[
            pltpu.VMEM((seq, LANES), BF16),
            pltpu.VMEM((seq, LANES), BF16),
            pltpu.VMEM((LANES, LANES), F32),
            pltpu.VMEM((2, n_blk, MOBA_BLOCK, MOBA_BLOCK), F32),
            pltpu.VMEM((2, MOBA_BLOCK, LANES), F32),
            pltpu.VMEM((2, MOBA_BLOCK, LANES), F32),
            pltpu.VMEM((2, MOBA_BLOCK, MOBA_BLOCK), BF16),
            pltpu.VMEM((2, MOBA_BLOCK, LANES), F32),
        ],
        compiler_params=_cparams(("parallel", "parallel", "arbitrary")),
        name="moba",
    )(hm, hm, hm, bias_tiles)


def _merge_kernel(x_ref, oa_ref, ob_ref, oc_ref, wg0_ref, wg1_ref, wg2_ref, wb_ref, wo_ref, g_ref, b_ref,
                  o_ref, xb_ref, acc_ref):
    j = pl.program_id(1)

    @pl.when(j == 0)
    def _():
        xb_ref[...] = x_ref[...].astype(BF16)
        acc_ref[...] = jnp.zeros(acc_ref.shape, F32)

    xb = xb_ref[...]
    merged = None
    for n, (o_n, wg_n) in enumerate(((oa_ref, wg0_ref), (ob_ref, wg1_ref), (oc_ref, wg2_ref))):
        gate = jax.nn.sigmoid(_dot(xb, wg_n[...]))
        y = _dot(o_n[...], wb_ref[n])
        merged = gate * y if merged is None else merged + gate * y
    acc_ref[...] += _dot(merged.astype(BF16), wo_ref[...])

    @pl.when(j == pl.num_programs(1) - 1)
    def _():
        z = DEEPNORM_ALPHA * x_ref[...] + acc_ref[...]
        o_ref[...] = _layer_norm(z, g_ref[...], b_ref[...])


def _merge(x2d, oa, ob, oc, wg, wb, wo, ln_g, ln_b, tm=512, tn=256):
    n, d = x2d.shape
    tm = min(tm, n)
    nd = d // tn
    bw = oa.shape[1]
    return pl.pallas_call(
        _merge_kernel,
        grid=(n // tm, nd),
        in_specs=[
            pl.BlockSpec((tm, d), lambda i, j: (i, 0)),
            pl.BlockSpec((tm, bw), lambda i, j: (i, 0)),
            pl.BlockSpec((tm, bw), lambda i, j: (i, 0)),
            pl.BlockSpec((tm, bw), lambda i, j: (i, 0)),
            pl.BlockSpec((d, tn), lambda i, j: (0, j)),
            pl.BlockSpec((d, tn), lambda i, j: (0, nd + j)),
            pl.BlockSpec((d, tn), lambda i, j: (0, 2 * nd + j)),
            pl.BlockSpec((N_BRANCH, bw, tn), lambda i, j: (0, 0, j)),
            pl.BlockSpec((tn, d), lambda i, j: (j, 0)),
            pl.BlockSpec((1, d), lambda i, j: (0, 0)),
            pl.BlockSpec((1, d), lambda i, j: (0, 0)),
        ],
        out_specs=pl.BlockSpec((tm, d), lambda i, j: (i, 0)),
        out_shape=jax.ShapeDtypeStruct((n, d), F32),
        scratch_shapes=[pltpu.VMEM((tm, d), BF16), pltpu.VMEM((tm, d), F32)],
        compiler_params=_cparams(("parallel", "arbitrary")),
        name="merge",
    )(x2d, oa, ob, oc, wg, wg, wg, wb, wo, ln_g.reshape(1, -1), ln_b.reshape(1, -1))


def _ffn_up_kernel(x_ref, xh_ref, wa_ref, wv_ref, cwa_ref, cwv_ref, cba_ref, cbv_ref, o_ref, xb_ref, xhb_ref,
                   *, tiles_per_seq):
    i = pl.program_id(0)
    j = pl.program_id(1)

    @pl.when(j == 0)
    def _():
        xb_ref[...] = x_ref[...].astype(BF16)
        has_prev = (i % tiles_per_seq != 0).astype(F32)
        xhb_ref[...] = (xh_ref[...] * has_prev).astype(BF16)

    xb = xb_ref[...]
    xhb = xhb_ref[...]
    tm, tn = o_ref.shape
    row = lax.broadcasted_iota(I32, (tm, tn), 0)

    def conv(w_ref, cw_ref, cb_ref):
        h = _dot(xb, w_ref[...])
        hh = _dot(xhb, w_ref[...])
        prev1 = jnp.where(row == 0, hh[7:8, :], pltpu.roll(h, 1, 0))
        prev2 = jnp.where(row == 0, hh[6:7, :], jnp.where(row == 1, hh[7:8, :], pltpu.roll(h, 2, 0)))
        return cb_ref[...] + cw_ref[0:1, :] * prev2 + cw_ref[1:2, :] * prev1 + cw_ref[2:3, :] * h

    a = conv(wa_ref, cwa_ref, cba_ref)
    val = conv(wv_ref, cwv_ref, cbv_ref)
    o_ref[...] = (jax.nn.gelu(a) * val).astype(o_ref.dtype)


def _ffn_up(x2d, seq, w_up, conv_w, conv_b, tm=512, tn=512):
    n, d = x2d.shape
    d_ff = w_up.shape[1] // 2
    tm = min(tm, seq)
    nf = d_ff // tn
    kernel = functools.partial(_ffn_up_kernel, tiles_per_seq=seq // tm)
    halo = tm // SUBLANES
    return pl.pallas_call(
        kernel,
        grid=(n // tm, nf),
        in_specs=[
            pl.BlockSpec((tm, d), lambda i, j: (i, 0)),
            pl.BlockSpec((SUBLANES, d), lambda i, j: (jnp.maximum(i * halo - 1, 0), 0)),
            pl.BlockSpec((d, tn), lambda i, j: (0, j)),
            pl.BlockSpec((d, tn), lambda i, j: (0, nf + j)),
            pl.BlockSpec((CONV_WIDTH, tn), lambda i, j: (0, j)),
            pl.BlockSpec((CONV_WIDTH, tn), lambda i, j: (0, nf + j)),
            pl.BlockSpec((1, tn), lambda i, j: (0, j)),
            pl.BlockSpec((1, tn), lambda i, j: (0, nf + j)),
        ],
        out_specs=pl.BlockSpec((tm, tn), lambda i, j: (i, j)),
        out_shape=jax.ShapeDtypeStruct((n, d_ff), BF16),
        scratch_shapes=[pltpu.VMEM((tm, d), BF16), pltpu.VMEM((SUBLANES, d), BF16)],
        compiler_params=_cparams(("parallel", "arbitrary")),
        name="ffn_up",
    )(x2d, x2d, w_up, w_up, conv_w, conv_w, conv_b.reshape(1, -1), conv_b.reshape(1, -1))


def _ffn_down_kernel(a_ref, w_ref, x_ref, g_ref, b_ref, o_ref, acc_ref):
    k = pl.program_id(1)

    @pl.when(k == 0)
    def _():
        acc_ref[...] = jnp.zeros(acc_ref.shape, F32)

    acc_ref[...] += _dot(a_ref[...], w_ref[...])

    @pl.when(k == pl.num_programs(1) - 1)
    def _():
        z = DEEPNORM_ALPHA * x_ref[...] + acc_ref[...]
        o_ref[...] = _layer_norm(z, g_ref[...], b_ref[...])


def _ffn_down(act, w_down, x2d, ln_g, ln_b, tm=512, tk=512):
    n, d = x2d.shape
    d_ff = act.shape[1]
    tm = min(tm, n)
    return pl.pallas_call(
        _ffn_down_kernel,
        grid=(n // tm, d_ff // tk),
        in_specs=[
            pl.BlockSpec((tm, tk), lambda i, k: (i, k)),
            pl.BlockSpec((tk, d), lambda i, k: (k, 0)),
            pl.BlockSpec((tm, d), lambda i, k: (i, 0)),
            pl.BlockSpec((1, d), lambda i, k: (0, 0)),
            pl.BlockSpec((1, d), lambda i, k: (0, 0)),
        ],
        out_specs=pl.BlockSpec((tm, d), lambda i, k: (i, 0)),
        out_shape=jax.ShapeDtypeStruct((n, d), F32),
        scratch_shapes=[pltpu.VMEM((tm, d), F32)],
        compiler_params=_cparams(("parallel", "arbitrary")),
        name="ffn_down",
    )(act, w_down, x2d, ln_g.reshape(1, -1), ln_b.reshape(1, -1))


def _mixer_weight(w_in_l):
    d = w_in_l.shape[0]
    sizes = (A_Q_RANK, A_KV_RANK, IDX_DIM, IDX_HEADS, SSM_WIDTH, 512, 512, 512)
    offs = np.cumsum((0,) + sizes)
    seg = [w_in_l[:, offs[i]:offs[i + 1]] for i in range(len(sizes))]
    pad = lambda w: jnp.zeros((d, w), w_in_l.dtype)
    cols = [seg[0], pad(CKV_OFF - A_Q_RANK), seg[1], seg[2], seg[3], pad(U_OFF - KW_OFF - IDX_DIM - IDX_HEADS),
            seg[4], seg[5], seg[6], seg[7]]
    w = jnp.concatenate(cols, axis=1)
    assert w.shape[1] == PROJ_W
    return w.astype(BF16), w_in_l[:, offs[-1]:].astype(BF16)


def kernel(x, rel_bias, w_in, cq_gain, ckv_gain, w_uq, w_uk, w_uv, w_qidx, lam_re, lam_im, log_step, b_re, b_im, c_re, c_im, d_skip, w_glu, b_glu, w_branch, w_out, ln1_g, ln1_b, w_up, conv_w, conv_b, w_down, ln2_g, ln2_b):
    bsz, seq, d = x.shape
    assert seq % MOBA_BLOCK == 0 and seq % DSA_KC == 0 and seq % S5_L == 0
    n = bsz * seq

    tab_t = jnp.transpose(rel_bias)
    dsa_bkt = jnp.asarray(_toeplitz_buckets(DSA_QB, DSA_KC, (0, DSA_QB, 2 * DSA_QB)))
    moba_bkt = jnp.asarray(_toeplitz_buckets(MOBA_BLOCK, MOBA_BLOCK, (0, MOBA_BLOCK)))
    dsa_bias = _bias_expand(dsa_bkt, tab_t[:A_HEADS]).reshape(A_HEADS, 4, DSA_QB, DSA_KC)
    dsa_bias = jnp.transpose(dsa_bias, (1, 0, 2, 3))
    moba_bias = _bias_expand(moba_bkt, tab_t[A_HEADS:]).reshape(C_HEADS, 3, MOBA_BLOCK, MOBA_BLOCK)
    moba_bias = jnp.transpose(moba_bias, (1, 0, 2, 3))

    x2d = x.reshape(n, d)
    for l in range(DEPTH):
        w_mix, w_gate = _mixer_weight(w_in[l])
        hm = _proj(x2d, w_mix).reshape(bsz, seq, PROJ_W)
        o_a = _dsa(hm, cq_gain[l], ckv_gain[l], w_uq[l], w_uk[l], w_uv[l], w_qidx[l], dsa_bias)
        tab, bb_re, bb_im = _s5_prep(lam_re[l], lam_im[l], log_step[l], b_re[l], b_im[l])
        o_b = _s5(hm, tab, bb_re, bb_im, c_re[l], c_im[l], d_skip[l], w_glu[l], b_glu[l])
        o_c = _moba(hm, moba_bias)
        bw = o_a.shape[-1]
        x2d = _merge(x2d, o_a.reshape(n, bw), o_b.reshape(n, bw), o_c.reshape(n, bw), w_gate,
                     w_branch[l].astype(BF16), w_out[l].astype(BF16), ln1_g[l], ln1_b[l])
        act = _ffn_up(x2d, seq, w_up[l].astype(BF16), conv_w[l], conv_b[l])
        x2d = _ffn_down(act, w_down[l].astype(BF16), x2d, ln2_g[l], ln2_b[l])
    return x2d.reshape(bsz, seq, d)
```

```python
import functools
import math

import numpy as np
import jax
import jax.numpy as jnp
from jax import lax
from jax.experimental import pallas as pl
from jax.experimental.pallas import tpu as pltpu

F32 = jnp.float32
BF16 = jnp.bfloat16
I32 = jnp.int32

DEPTH = 2
A_HEADS = 8
A_HEAD_DIM = 64
A_Q_RANK = 384
A_KV_RANK = 256
IDX_HEADS = 16
IDX_DIM = 64
DSA_TOPK = 256
SSM_GROUP = 16
SSM_GROUPS = 32
SSM_WIDTH = SSM_GROUP * SSM_GROUPS
SSM_STATE = 64
C_HEADS = 8
C_HEAD_DIM = 64
MOBA_BLOCK = 256
MOBA_TOPK = 3
N_BRANCH = 3
BRANCH_WIDTH = 512
CONV_WIDTH = 3
REL_BUCKETS = 32
REL_MAX_DIST = 128
LN_EPS = 1e-5
NEG_INF = -1e30
DEEPNORM_ALPHA = (2 * DEPTH) ** 0.25

LANES = 128
SUBLANES = 8
VMEM_LIMIT = 56 * 1024 * 1024

CQ_OFF, CKV_OFF, KW_OFF, U_OFF, Q_OFF, K_OFF, V_OFF, PROJ_W = 0, 512, 768, 1024, 1536, 2048, 2560, 3072

DSA_QB = 128
DSA_KC = 256
S5_L = 256
NS = SSM_GROUPS * SSM_STATE


def _cparams(sem):
    return pltpu.CompilerParams(dimension_semantics=sem, vmem_limit_bytes=VMEM_LIMIT)


def _dot(a, b):
    return jnp.dot(a, b, preferred_element_type=F32)


def _dot_nt(a, b):
    return lax.dot_general(a, b, (((1,), (1,)), ((), ())), preferred_element_type=F32)


def _order_key(s):
    bits = lax.bitcast_convert_type(s, I32)
    return bits ^ ((bits >> 31) & 0x7FFFFFFF)


def _layer_norm(z, g, b):
    mu = jnp.mean(z, axis=-1, keepdims=True)
    zc = z - mu
    var = jnp.mean(zc * zc, axis=-1, keepdims=True)
    return zc * lax.rsqrt(var + LN_EPS) * g + b


def _proj_kernel(x_ref, w_ref, o_ref):
    o_ref[...] = _dot(x_ref[...], w_ref[...])


def _proj(xb, w, tm=1024, tn=512):
    n, d = xb.shape
    pw = w.shape[1]
    tm = min(tm, n)
    return pl.pallas_call(
        _proj_kernel,
        grid=(n // tm, pw // tn),
        in_specs=[pl.BlockSpec((tm, d), lambda i, j: (i, 0)),
                  pl.BlockSpec((d, tn), lambda i, j: (0, j))],
        out_specs=pl.BlockSpec((tm, tn), lambda i, j: (i, j)),
        out_shape=jax.ShapeDtypeStruct((n, pw), F32),
        compiler_params=_cparams(("arbitrary", "arbitrary")),
        name="proj",
    )(xb, w)


def _bias_kernel(bkt_ref, tab_ref, o_ref):
    b = bkt_ref[...]
    acc = jnp.zeros(o_ref.shape, F32)
    for k in range(REL_BUCKETS):
        acc = jnp.where(b == k, tab_ref[:, k:k + 1], acc)
    o_ref[...] = acc


def _bias_expand(bucket, tab_t, pc=8192):
    p = bucket.shape[1]
    h = tab_t.shape[0]
    return pl.pallas_call(
        _bias_kernel,
        grid=(p // pc,),
        in_specs=[pl.BlockSpec((1, pc), lambda i: (0, i)),
                  pl.BlockSpec((h, REL_BUCKETS), lambda i: (0, 0))],
        out_specs=pl.BlockSpec((h, pc), lambda i: (0, i)),
        out_shape=jax.ShapeDtypeStruct((h, p), F32),
        compiler_params=_cparams(("arbitrary",)),
        name="bias_expand",
    )(bucket, tab_t)


def _t5_bucket_np(dist):
    n = np.maximum(dist, 0)
    exact = REL_BUCKETS // 2
    log_ratio = np.log(np.maximum(n, 1).astype(np.float32) / exact) / math.log(REL_MAX_DIST / exact)
    large = np.minimum(exact + (log_ratio * (REL_BUCKETS - exact)).astype(np.int32), REL_BUCKETS - 1)
    return np.where(n < exact, n, large).astype(np.int32)


def _toeplitz_buckets(rows, cols, deltas):
    i = np.arange(rows)[:, None]
    j = np.arange(cols)[None, :]
    tiles = [_t5_bucket_np(d + i - j) for d in deltas]
    tiles.append(np.full((rows, cols), REL_BUCKETS - 1, np.int32))
    return np.stack(tiles).reshape(1, -1)


def _dsa_kernel(cq_ref, ckv_ref, kw_ref, cqg_ref, ckvg_ref, wuq_ref, wuk_ref, wqi_ref, wuv_ref, bias_ref,
                o_ref, kvn_ref, kk_ref, key_ref, key_t_ref, ql_ref, lg_ref, mx_ref, ls_ref, p_ref, acc_ref, *, seq, n_top):
    qb = pl.program_id(1)
    q0 = qb * DSA_QB
    n_chunk_rows = seq // DSA_KC

    @pl.when(qb == 0)
    def _():
        lane = lax.broadcasted_iota(I32, (DSA_KC, LANES), 1)

        def prep(c, carry):
            rows = pl.ds(pl.multiple_of(c * DSA_KC, DSA_KC), DSA_KC)
            ckv = ckv_ref[rows, :]
            ms = jnp.mean(ckv * ckv, axis=-1, keepdims=True)
            kvn_ref[rows, :] = (ckv * lax.rsqrt(ms + LN_EPS) * ckvg_ref[...]).astype(BF16)
            kw = kw_ref[rows, :]
            kk_ref[rows, :] = jnp.where(lane < IDX_DIM, kw, pltpu.roll(kw, IDX_DIM, 1)).astype(BF16)
            return carry

        lax.fori_loop(0, n_chunk_rows, prep, 0)

    nc = (q0 + DSA_QB + DSA_KC - 1) // DSA_KC

    cq = cq_ref[...]
    ms = jnp.mean(cq * cq, axis=-1, keepdims=True)
    cqn = (cq * lax.rsqrt(ms + LN_EPS) * cqg_ref[...]).astype(BF16)

    lane_q = lax.broadcasted_iota(I32, (DSA_QB, LANES), 1)
    lo_half = lane_q < IDX_DIM
    row_i = lax.broadcasted_iota(I32, (DSA_QB, DSA_KC), 0)
    col_j = lax.broadcasted_iota(I32, (DSA_QB, DSA_KC), 1)
    qpos = q0 + row_i

    qi = _dot(cqn, wqi_ref[...]) * (IDX_DIM ** -0.5)
    wq = kw_ref[pl.ds(pl.multiple_of(q0, DSA_QB), DSA_QB), :] * (IDX_HEADS ** -0.5)
    q_heads = []
    for h in range(IDX_HEADS):
        qp = qi[:, (h // 2) * LANES:(h // 2 + 1) * LANES]
        keep = lo_half if h % 2 == 0 else jnp.logical_not(lo_half)
        q_heads.append(jnp.where(keep, qp, 0.0).astype(BF16))

    def score_body(c, carry):
        kc = kk_ref[pl.ds(pl.multiple_of(c * DSA_KC, DSA_KC), DSA_KC), :]
        s = jnp.zeros((DSA_QB, DSA_KC), F32)
        for h in range(IDX_HEADS):
            r = _dot_nt(q_heads[h], kc)
            s = s + wq[:, IDX_DIM + h:IDX_DIM + h + 1] * jnp.maximum(r, 0.0)
        s = jnp.where(qpos >= c * DSA_KC + col_j, s, -jnp.inf)
        key_ref[c] = _order_key(s)
        key_t_ref[c] = _order_key(s.T).reshape(DSA_KC // SUBLANES, SUBLANES, DSA_QB)
        return carry

    lax.fori_loop(0, nc, score_body, 0)

    n_grp = DSA_KC // SUBLANES
    kpos_t = (lax.broadcasted_iota(I32, (n_grp, SUBLANES, DSA_QB), 0) * SUBLANES
              + lax.broadcasted_iota(I32, (n_grp, SUBLANES, DSA_QB), 1))

    def count(pred):
        def body(c, tot):
            one = jnp.where(pred(key_t_ref[c], c), 1.0, 0.0)
            parts = [one[g] for g in range(n_grp)]
            while len(parts) > 1:
                parts = [parts[i] + parts[i + 1] for i in range(0, len(parts), 2)]
            return tot + parts[0]

        tot = lax.fori_loop(0, nc, body, jnp.zeros((SUBLANES, DSA_QB), F32))
        for d in (4, 2, 1):
            tot = tot + pltpu.roll(tot, d, 0)
        return tot

    int_min = jnp.int32(-2 ** 31)
    k_top = float(n_top)
    cnt = count(lambda k, c: k >= 0)
    thr0 = jnp.where(cnt >= k_top, jnp.int32(0), int_min)

    def bit_body(i, thr):
        cand = thr | jnp.left_shift(jnp.int32(1), 30 - i)
        cnt = count(lambda k, c: k >= cand)
        return jnp.where(cnt >= k_top, cand, thr)

    thr_t = lax.fori_loop(0, 31, bit_body, thr0)

    need = k_top - count(lambda k, c: k > thr_t)
    n_eq = count(lambda k, c: k == thr_t)
    any_tie = jnp.max(jnp.where(n_eq > need, 1.0, 0.0)) > 0.0
    idx_bits = int(math.log2(seq))

    def tie_cut():
        def body(i, v):
            cand = v | jnp.left_shift(jnp.int32(1), idx_bits - 1 - i)
            cnt = count(lambda k, c: (k == thr_t) & (c * DSA_KC + kpos_t < cand))
            return jnp.where(cnt < need, cand, v)

        return lax.fori_loop(0, idx_bits, body, jnp.zeros((SUBLANES, DSA_QB), I32))

    cut_t = lax.cond(any_tie, tie_cut, lambda: jnp.full((SUBLANES, DSA_QB), seq, I32))

    def to_rows(v_t):
        def col(x):
            return jnp.broadcast_to(x.astype(F32), (DSA_QB, DSA_QB)).T.astype(I32)

        v = v_t[0:1, :]
        r = (col(v >> 16) << 16) | col(v & 0xFFFF)
        return jnp.concatenate([r] * (DSA_KC // DSA_QB), axis=1)

    thr = to_rows(thr_t)
    cut = to_rows(cut_t)

    q = _dot(cqn, wuq_ref[...])
    for h in range(A_HEADS):
        qp = q[:, (h // 2) * LANES:(h // 2 + 1) * LANES]
        keep = lo_half if h % 2 == 0 else jnp.logical_not(lo_half)
        qm = jnp.where(keep, qp, 0.0).astype(BF16)
        ql_ref[h * DSA_QB:(h + 1) * DSA_QB, :] = (_dot(qm, wuk_ref[h // 2]) * (A_HEAD_DIM ** -0.5)).astype(BF16)
    mx_ref[...] = jnp.full(mx_ref.shape, NEG_INF, F32)
    ls_ref[...] = jnp.zeros(ls_ref.shape, F32)
    acc_ref[...] = jnp.zeros(acc_ref.shape, F32)

    def logit_body(c, carry):
        kv = kvn_ref[pl.ds(pl.multiple_of(c * DSA_KC, DSA_KC), DSA_KC), :]
        k = key_ref[c]
        kpos = c * DSA_KC + col_j
        sel = ((k > thr) | ((k == thr) & (kpos <= cut))) & (qpos >= kpos)
        add_mask = jnp.where(sel, 0.0, NEG_INF)
        t = jnp.minimum((q0 - c * DSA_KC) // DSA_QB, 3)
        s_all = _dot_nt(ql_ref[...], kv)
        for h in range(A_HEADS):
            rows = slice(h * DSA_QB, (h + 1) * DSA_QB)
            sm = s_all[rows, :] + bias_ref[t, h] + add_mask
            lg_ref[c, rows, :] = sm
            mx_ref[h] = jnp.maximum(mx_ref[h], jnp.maximum(sm[:, :LANES], sm[:, LANES:]))
        return carry

    lax.fori_loop(0, nc, logit_body, 0)

    for h in range(A_HEADS):
        mx_ref[h] = jnp.broadcast_to(jnp.max(mx_ref[h], axis=1, keepdims=True), (DSA_QB, LANES))

    def pv_body(c, carry):
        kv = kvn_ref[pl.ds(pl.multiple_of(c * DSA_KC, DSA_KC), DSA_KC), :]
        for h in range(A_HEADS):
            rows = slice(h * DSA_QB, (h + 1) * DSA_QB)
            m = mx_ref[h]
            p_lo = jnp.exp(lg_ref[c, rows, :LANES] - m)
            p_hi = jnp.exp(lg_ref[c, rows, LANES:] - m)
            ls_ref[h] += p_lo + p_hi
            p_ref[rows, :LANES] = p_lo.astype(BF16)
            p_ref[rows, LANES:] = p_hi.astype(BF16)
        acc_ref[...] += _dot(p_ref[...], kv)
        return carry

    lax.fori_loop(0, nc, pv_body, 0)

    for pr in range(A_HEADS // 2):
        o = jnp.zeros((DSA_QB, LANES), F32)
        for h in (2 * pr, 2 * pr + 1):
            l = jnp.sum(ls_ref[h], axis=1, keepdims=True)
            o_lat = (acc_ref[h * DSA_QB:(h + 1) * DSA_QB, :] / l).astype(BF16)
            o = o + _dot(o_lat, wuv_ref[h])
        o_ref[:, pr * LANES:(pr + 1) * LANES] = o.astype(o_ref.dtype)


def _dsa(hm, cq_gain, ckv_gain, w_uq, w_uk, w_uv, w_qidx, bias_tiles):
    bsz, seq, _ = hm.shape
    n_top = min(DSA_TOPK, seq // 4)
    hd = A_HEADS * A_HEAD_DIM
    wuq = w_uq.reshape(A_Q_RANK, hd).astype(BF16)
    wuk = jnp.transpose(w_uk, (1, 2, 0)).reshape(A_HEADS // 2, 2 * A_HEAD_DIM, A_KV_RANK).astype(BF16)
    wqi = w_qidx.reshape(A_Q_RANK, IDX_HEADS * IDX_DIM).astype(BF16)
    wv = jnp.transpose(w_uv, (1, 0, 2))
    zeros = jnp.zeros_like(wv)
    even = (jnp.arange(A_HEADS) % 2 == 0)[:, None, None]
    wuv = jnp.concatenate([jnp.where(even, wv, zeros), jnp.where(even, zeros, wv)], axis=-1).astype(BF16)
    kernel = functools.partial(_dsa_kernel, seq=seq, n_top=n_top)
    const2 = lambda b, q: (0, 0)
    const3 = lambda b, q: (0, 0, 0)
    return pl.pallas_call(
        kernel,
        grid=(bsz, seq // DSA_QB),
        in_specs=[
            pl.BlockSpec((None, DSA_QB, A_Q_RANK), lambda b, q: (b, q, CQ_OFF // A_Q_RANK)),
            pl.BlockSpec((None, seq, A_KV_RANK), lambda b, q: (b, 0, CKV_OFF // A_KV_RANK)),
            pl.BlockSpec((None, seq, LANES), lambda b, q: (b, 0, KW_OFF // LANES)),
            pl.BlockSpec((1, A_Q_RANK), const2),
            pl.BlockSpec((1, A_KV_RANK), const2),
            pl.BlockSpec(wuq.shape, const2),
            pl.BlockSpec(wuk.shape, const3),
            pl.BlockSpec(wqi.shape, const2),
            pl.BlockSpec(wuv.shape, const3),
            pl.BlockSpec(bias_tiles.shape, lambda b, q: (0, 0, 0, 0)),
        ],
        out_specs=pl.BlockSpec((None, DSA_QB, hd), lambda b, q: (b, q, 0)),
        out_shape=jax.ShapeDtypeStruct((bsz, seq, hd), BF16),
        scratch_shapes=[
            pltpu.VMEM((seq, A_KV_RANK), BF16),
            pltpu.VMEM((seq, LANES), BF16),
            pltpu.VMEM((seq // DSA_KC, DSA_QB, DSA_KC), I32),
            pltpu.VMEM((seq // DSA_KC, DSA_KC // SUBLANES, SUBLANES, DSA_QB), I32),
            pltpu.VMEM((A_HEADS * DSA_QB, A_KV_RANK), BF16),
            pltpu.VMEM((seq // DSA_KC, A_HEADS * DSA_QB, DSA_KC), F32),
            pltpu.VMEM((A_HEADS, DSA_QB, LANES), F32),
            pltpu.VMEM((A_HEADS, DSA_QB, LANES), F32),
            pltpu.VMEM((A_HEADS * DSA_QB, DSA_KC), BF16),
            pltpu.VMEM((A_HEADS * DSA_QB, A_KV_RANK), F32),
        ],
        compiler_params=_cparams(("arbitrary", "arbitrary")),
        name="dsa",
    )(hm, hm, hm, cq_gain.reshape(1, -1), ckv_gain.reshape(1, -1), wuq, wuk, wqi, wuv, bias_tiles)


S5_POWERS = (1, 2, 4) + tuple(range(1, SUBLANES + 1))
S5_ROWS = 2 * len(S5_POWERS) + 2


def _s5_prep_kernel(lre_ref, lim_ref, ls_ref, bre_ref, bim_ref, tab_ref, bbre_ref, bbim_ref):
    lre = lre_ref[...]
    lim = lim_ref[...]
    step = jnp.exp(ls_ref[...])
    for r, k in enumerate(S5_POWERS):
        mag = jnp.exp(k * lre * step)
        ang = k * lim * step
        tab_ref[2 * r:2 * r + 1, :] = mag * jnp.cos(ang)
        tab_ref[2 * r + 1:2 * r + 2, :] = mag * jnp.sin(ang)
    a = tab_ref[0:1, :] - 1.0
    b = tab_ref[1:2, :]
    den = lre * lre + lim * lim
    f_re = (a * lre + b * lim) / den
    f_im = (b * lre - a * lim) / den
    base = 2 * len(S5_POWERS)
    tab_ref[base:base + 1, :] = f_re
    tab_ref[base + 1:base + 2, :] = f_im
    bre = bre_ref[...]
    bim = bim_ref[...]
    bbre_ref[...] = f_re * bre - f_im * bim
    bbim_ref[...] = f_re * bim + f_im * bre


def _s5_prep(lam_re, lam_im, log_step, b_re, b_im):
    lre = lam_re.reshape(1, NS)
    lim = lam_im.reshape(1, NS)
    ls = jnp.repeat(log_step, SSM_STATE).reshape(1, NS)
    bre = jnp.transpose(b_re, (2, 0, 1)).reshape(SSM_GROUP, NS)
    bim = jnp.transpose(b_im, (2, 0, 1)).reshape(SSM_GROUP, NS)
    return pl.pallas_call(
        _s5_prep_kernel,
        out_shape=(jax.ShapeDtypeStruct((S5_ROWS, NS), F32),
                   jax.ShapeDtypeStruct((SSM_GROUP, NS), F32),
                   jax.ShapeDtypeStruct((SSM_GROUP, NS), F32)),
        name="s5_prep",
    )(lre, lim, ls, bre, bim)


def _s5_kernel(u_ref, bm_ref, cm_ref, lk_ref, pw_ref, dsk_ref, wg_ref, bg_ref, o_ref, xs_ref, c_ref, *, lane_chunk):
    tc = pl.program_id(1)

    @pl.when(tc == 0)
    def _():
        c_ref[...] = jnp.zeros(c_ref.shape, F32)

    u = u_ref[...]
    xs_ref[...] = _dot(u.astype(BF16), bm_ref[...])

    n_tiles = S5_L // SUBLANES
    for lc in range(NS // lane_chunk):
        re_cols = slice(lc * lane_chunk, (lc + 1) * lane_chunk)
        im_cols = slice(NS + lc * lane_chunk, NS + (lc + 1) * lane_chunk)
        l_re = [lk_ref[k, 0, :, re_cols] for k in range(3)]
        l_im = [lk_ref[k, 1, :, re_cols] for k in range(3)]
        p_re = pw_ref[0, :, re_cols]
        p_im = pw_ref[1, :, re_cols]

        def tile_body(t, carry, re_cols=re_cols, im_cols=im_cols, l_re=l_re, l_im=l_im, p_re=p_re, p_im=p_im):
            c_re, c_im = carry
            rows = pl.ds(pl.multiple_of(t * SUBLANES, SUBLANES), SUBLANES)
            re = xs_ref[rows, re_cols]
            im = xs_ref[rows, im_cols]
            for k, d in enumerate((1, 2, 4)):
                s_re = pltpu.roll(re, d, 0)
                s_im = pltpu.roll(im, d, 0)
                re, im = (re + l_re[k] * s_re - l_im[k] * s_im,
                          im + l_re[k] * s_im + l_im[k] * s_re)
            re, im = (re + p_re * c_re - p_im * c_im,
                      im + p_re * c_im + p_im * c_re)
            xs_ref[rows, re_cols] = re
            xs_ref[rows, im_cols] = im
            last = SUBLANES - 1
            return (jnp.broadcast_to(re[last:last + 1, :], re.shape),
                    jnp.broadcast_to(im[last:last + 1, :], im.shape))

        c_re, c_im = lax.fori_loop(0, n_tiles, tile_body, (c_ref[:, re_cols], c_ref[:, im_cols]))
        c_ref[:, re_cols] = c_re
        c_ref[:, im_cols] = c_im

    y = _dot(xs_ref[...].astype(BF16), cm_ref[...]) + dsk_ref[...] * u
    y = jax.nn.gelu(y)
    z = _dot(y.astype(BF16), wg_ref[...]) + bg_ref[...]
    o_ref[...] = (y * jax.nn.sigmoid(z)).astype(o_ref.dtype)


def _s5(hm, tab, bb_re, bb_im, c_re, c_im, d_skip, w_glu, b_glu, lane_chunk=512):
    bsz, seq, _ = hm.shape
    eye = jnp.eye(SSM_GROUPS, dtype=F32)

    def in_map(bb):
        bb = jnp.transpose(bb.reshape(SSM_GROUP, SSM_GROUPS, SSM_STATE), (1, 0, 2))
        return jnp.einsum('gpn,gh->gphn', bb, eye).reshape(SSM_WIDTH, NS)

    def out_map(c):
        return jnp.einsum('gpn,gh->gnhp', c, eye).reshape(NS, SSM_WIDTH)

    bm = jnp.concatenate([in_map(bb_re), in_map(bb_im)], axis=1).astype(BF16)
    cm = jnp.concatenate([out_map(c_re), -out_map(c_im)], axis=0).astype(BF16)
    row = jnp.arange(SUBLANES)[:, None]
    lk = jnp.stack([jnp.stack([jnp.where(row >= d, tab[2 * k], 0.0), jnp.where(row >= d, tab[2 * k + 1], 0.0)])
                    for k, d in enumerate((1, 2, 4))])
    pw = jnp.stack([tab[6:6 + 2 * SUBLANES:2], tab[7:7 + 2 * SUBLANES:2]])
    kernel = functools.partial(_s5_kernel, lane_chunk=lane_chunk)
    const2 = lambda b, t: (0, 0)
    return pl.pallas_call(
        kernel,
        grid=(bsz, seq // S5_L),
        in_specs=[
            pl.BlockSpec((None, S5_L, SSM_WIDTH), lambda b, t: (b, t, U_OFF // SSM_WIDTH)),
            pl.BlockSpec(bm.shape, const2),
            pl.BlockSpec(cm.shape, const2),
            pl.BlockSpec(lk.shape, lambda b, t: (0, 0, 0, 0)),
            pl.BlockSpec(pw.shape, lambda b, t: (0, 0, 0)),
            pl.BlockSpec((1, SSM_WIDTH), const2),
            pl.BlockSpec((SSM_WIDTH, SSM_WIDTH), const2),
            pl.BlockSpec((1, SSM_WIDTH), const2),
        ],
        out_specs=pl.BlockSpec((None, S5_L, SSM_WIDTH), lambda b, t: (b, t, 0)),
        out_shape=jax.ShapeDtypeStruct((bsz, seq, SSM_WIDTH), BF16),
        scratch_shapes=[pltpu.VMEM((S5_L, 2 * NS), F32),
                        pltpu.VMEM((SUBLANES, 2 * NS), F32)],
        compiler_params=_cparams(("arbitrary", "arbitrary")),
        name="s5",
    )(hm, bm, cm, lk, pw, d_skip.reshape(1, -1), w_glu.astype(BF16), b_glu.reshape(1, -1))


def _moba_kernel(q_ref, k_ref, v_ref, bias_ref, o_ref, kb_ref, vb_ref, km_ref, lg_ref, mx_ref, ls_ref, p_ref, acc_ref,
                 *, n_blk):
    qb = pl.program_id(2)
    blk = MOBA_BLOCK

    @pl.when(qb == 0)
    def _():
        km_ref[...] = jnp.zeros(km_ref.shape, F32)
        lane_k = lax.broadcasted_iota(I32, (blk, LANES), 1)
        for n in range(n_blk):
            rows = slice(n * blk, (n + 1) * blk)
            kblk = k_ref[rows, :]
            km_ref[n:n + 1, :] = jnp.mean(kblk, axis=0, keepdims=True)
            kb_ref[0, rows, :] = jnp.where(lane_k < C_HEAD_DIM, kblk,
                                           jnp.where(lane_k == C_HEAD_DIM + n, 1.0, 0.0)).astype(BF16)
            kb_ref[1, rows, :] = jnp.where(lane_k >= C_HEAD_DIM, kblk, jnp.where(lane_k == n, 1.0, 0.0)).astype(BF16)
            vb_ref[rows, :] = v_ref[rows, :].astype(BF16)

    q = q_ref[...]
    lane = lax.broadcasted_iota(I32, (blk, LANES), 1)
    row_i = lax.broadcasted_iota(I32, (blk, blk), 0)
    col_j = lax.broadcasted_iota(I32, (blk, blk), 1)
    causal_add = jnp.where(row_i >= col_j, 0.0, NEG_INF)
    blk_id = lax.broadcasted_iota(I32, (SUBLANES, blk), 0)
    blk_f = blk_id.astype(F32)
    km = km_ref[...]
    lane_m = lax.broadcasted_iota(I32, (SUBLANES, LANES), 1)
    km_both = jnp.concatenate([jnp.where(lane_m < C_HEAD_DIM, km, 0.0), jnp.where(lane_m >= C_HEAD_DIM, km, 0.0)], axis=0)
    km_hi = km_both.astype(BF16)
    km_lo = (km_both - km_hi.astype(F32)).astype(BF16)
    q_hi = q.astype(BF16)
    q_lo = (q - q_hi.astype(F32)).astype(BF16)
    gate_both = _dot_nt(km_hi, q_hi) + (_dot_nt(km_hi, q_lo) + _dot_nt(km_lo, q_hi))
    scale = C_HEAD_DIM ** -0.5
    own_rows = pl.ds(pl.multiple_of(qb * blk, blk), blk)

    def over_blocks(x, op):
        for d in (4, 2, 1):
            x = op(x, pltpu.roll(x, d, 0))
        return x

    qa = []
    for s in range(2):
        keep = (lane < C_HEAD_DIM) if s == 0 else (lane >= C_HEAD_DIM)
        gate = gate_both[s * SUBLANES:(s + 1) * SUBLANES, :]
        g = jnp.where(blk_id < qb, gate, -jnp.inf)
        sel = jnp.zeros((SUBLANES, blk), F32)
        for _ in range(MOBA_TOPK):
            mx = over_blocks(g, jnp.maximum)
            is_max = (g == mx) & (g > -jnp.inf)
            first = over_blocks(jnp.where(is_max, blk_f, float(SUBLANES)), jnp.minimum)
            pick = blk_f == first
            sel = jnp.where(pick, 1.0, sel)
            g = jnp.where(pick, -jnp.inf, g)
        add_t = jnp.where((sel > 0.5) | (blk_id == qb), 0.0, NEG_INF)
        lead = C_HEAD_DIM if s == 0 else 0
        pieces = ([jnp.zeros((lead, blk), F32)] if lead else []) + [add_t]
        pieces.append(jnp.zeros((LANES - lead - SUBLANES, blk), F32))
        add_q = jnp.concatenate(pieces, axis=0).T
        qa.append(jnp.where(keep, q * scale, add_q).astype(BF16))
        sm = _dot_nt(qa[s], kb_ref[s, own_rows, :]) + bias_ref[0, s] + causal_add
        lg_ref[s, qb] = sm
        mx_ref[s] = jnp.maximum(sm[:, :LANES], sm[:, LANES:])
        ls_ref[s] = jnp.zeros((blk, LANES), F32)
        acc_ref[s] = jnp.zeros((blk, LANES), F32)

    def logit_body(n, carry):
        rows = pl.ds(pl.multiple_of(n * blk, blk), blk)
        t = jnp.minimum(qb - n, 2)
        for s in range(2):
            sm = _dot_nt(qa[s], kb_ref[s, rows, :]) + bias_ref[t, s]
            lg_ref[s, n] = sm
            mx_ref[s] = jnp.maximum(mx_ref[s], jnp.maximum(sm[:, :LANES], sm[:, LANES:]))
        return carry

    lax.fori_loop(0, qb, logit_body, 0)

    for s in range(2):
        mx_ref[s] = jnp.broadcast_to(jnp.max(mx_ref[s], axis=1, keepdims=True), (blk, LANES))

    def pv_body(n, carry):
        vc = vb_ref[pl.ds(pl.multiple_of(n * blk, blk), blk), :]
        for s in range(2):
            m = mx_ref[s]
            p_lo = jnp.exp(lg_ref[s, n, :, :LANES] - m)
            p_hi = jnp.exp(lg_ref[s, n, :, LANES:] - m)
            ls_ref[s] += p_lo + p_hi
            p_ref[s, :, :LANES] = p_lo.astype(BF16)
            p_ref[s, :, LANES:] = p_hi.astype(BF16)
            acc_ref[s] += _dot(p_ref[s], vc)
        return carry

    lax.fori_loop(0, qb + 1, pv_body, 0)

    o = jnp.where(lane < C_HEAD_DIM,
                  acc_ref[0] / jnp.sum(ls_ref[0], axis=1, keepdims=True),
                  acc_ref[1] / jnp.sum(ls_ref[1], axis=1, keepdims=True))
    o_ref[...] = o.astype(o_ref.dtype)


def _moba(hm, bias_tiles):
    bsz, seq, _ = hm.shape
    n_blk = seq // MOBA_BLOCK
    n_pair = C_HEADS // 2
    assert n_blk <= SUBLANES
    kernel = functools.partial(_moba_kernel, n_blk=n_blk)
    return pl.pallas_call(
        kernel,
        grid=(bsz, n_pair, n_blk),
        in_specs=[
            pl.BlockSpec((None, MOBA_BLOCK, LANES), lambda b, h, q: (b, q, Q_OFF // LANES + h)),
            pl.BlockSpec((None, seq, LANES), lambda b, h, q: (b, 0, K_OFF // LANES + h)),
            pl.BlockSpec((None, seq, LANES), lambda b, h, q: (b, 0, V_OFF // LANES + h)),
            pl.BlockSpec((3, 2, MOBA_BLOCK, MOBA_BLOCK), lambda b, h, q: (0, h, 0, 0)),
        ],
        out_specs=pl.BlockSpec((None, MOBA_BLOCK, LANES), lambda b, h, q: (b, q, h)),
        out_shape=jax.ShapeDtypeStruct((bsz, seq, C_HEADS * C_HEAD_DIM), BF16),
        scratch_shapes=[
            pltpu.VMEM((2, seq, LANES), BF16),
            pltpu.VMEM((seq, LANES), BF16),
            pltpu.VMEM((SUBLANES, LANES), F32),
            pltpu.VMEM((2, n_blk, MOBA_BLOCK, MOBA_BLOCK), F32),
            pltpu.VMEM((2, MOBA_BLOCK, LANES), F32),
            pltpu.VMEM((2, MOBA_BLOCK, LANES), F32),
            pltpu.VMEM((2, MOBA_BLOCK, MOBA_BLOCK), BF16),
            pltpu.VMEM((2, MOBA_BLOCK, LANES), F32),
        ],
        compiler_params=_cparams(("arbitrary", "arbitrary", "arbitrary")),
        name="moba",
    )(hm, hm, hm, bias_tiles)


def _merge_kernel(xb_ref, x_ref, oa_ref, ob_ref, oc_ref, wg0_ref, wg1_ref, wg2_ref, wb_ref, wo_ref, g_ref, b_ref,
                  o_ref, o16_ref, mg_ref):
    j = pl.program_id(1)
    tn = wg0_ref.shape[1]
    xb = xb_ref[...]
    merged = None
    for n, (o_n, wg_n) in enumerate(((oa_ref, wg0_ref), (ob_ref, wg1_ref), (oc_ref, wg2_ref))):
        gate = jax.nn.sigmoid(_dot(xb, wg_n[...]))
        y = _dot(o_n[...], wb_ref[n])
        merged = gate * y if merged is None else merged + gate * y
    mg_ref[:, pl.ds(pl.multiple_of(j * tn, tn), tn)] = merged.astype(BF16)

    @pl.when(j == pl.num_programs(1) - 1)
    def _():
        z = DEEPNORM_ALPHA * x_ref[...] + _dot(mg_ref[...], wo_ref[...])
        y = _layer_norm(z, g_ref[...], b_ref[...])
        o_ref[...] = y
        o16_ref[...] = y.astype(BF16)


def _merge(xb, x2d, oa, ob, oc, wg, wb, wo, ln_g, ln_b, tm=512, tn=256):
    n, d = x2d.shape
    tm = min(tm, n)
    nd = d // tn
    bw = oa.shape[1]
    row = lambda i, j: (i, 0)
    const2 = lambda i, j: (0, 0)
    return pl.pallas_call(
        _merge_kernel,
        grid=(n // tm, nd),
        in_specs=[
            pl.BlockSpec((tm, d), row),
            pl.BlockSpec((tm, d), row),
            pl.BlockSpec((tm, bw), row),
            pl.BlockSpec((tm, bw), row),
            pl.BlockSpec((tm, bw), row),
            pl.BlockSpec((d, tn), lambda i, j: (0, j)),
            pl.BlockSpec((d, tn), lambda i, j: (0, nd + j)),
            pl.BlockSpec((d, tn), lambda i, j: (0, 2 * nd + j)),
            pl.BlockSpec((N_BRANCH, bw, tn), lambda i, j: (0, 0, j)),
            pl.BlockSpec((d, d), const2, pipeline_mode=pl.Buffered(1)),
            pl.BlockSpec((1, d), const2),
            pl.BlockSpec((1, d), const2),
        ],
        out_specs=[pl.BlockSpec((tm, d), row), pl.BlockSpec((tm, d), row)],
        out_shape=[jax.ShapeDtypeStruct((n, d), F32), jax.ShapeDtypeStruct((n, d), BF16)],
        scratch_shapes=[pltpu.VMEM((tm, d), BF16)],
        compiler_params=_cparams(("arbitrary", "arbitrary")),
        name="merge",
    )(xb, x2d, oa, ob, oc, wg, wg, wg, wb, wo, ln_g.reshape(1, -1), ln_b.reshape(1, -1))


HALO = 16


def _ffn_up_kernel(x_ref, xh_ref, wa_ref, wv_ref, cwa_ref, cwv_ref, cba_ref, cbv_ref, o_ref, *, tiles_per_seq):
    i = pl.program_id(1)
    has_prev = (i % tiles_per_seq != 0).astype(F32)
    xb = x_ref[...]
    xh = xh_ref[...]
    row = lax.broadcasted_iota(I32, (SUBLANES, o_ref.shape[1]), 0)

    def conv(w_ref, cw_ref, cb_ref):
        h = _dot(xb, w_ref[...])
        hh = _dot(xh, w_ref[...]) * has_prev
        w0, w1, w2, cb = cw_ref[0:1, :], cw_ref[1:2, :], cw_ref[2:3, :], cb_ref[...]
        body = cb + w0 * pltpu.roll(h, 2, 0) + w1 * pltpu.roll(h, 1, 0) + w2 * h
        h8 = h[0:SUBLANES, :]
        p1 = jnp.where(row == 0, hh[HALO - 1:HALO, :], pltpu.roll(h8, 1, 0))
        p2 = jnp.where(row == 0, hh[HALO - 2:HALO - 1, :],
                       jnp.where(row == 1, hh[HALO - 1:HALO, :], pltpu.roll(h8, 2, 0)))
        head = cb + w0 * p2 + w1 * p1 + w2 * h8
        return jnp.concatenate([head, body[SUBLANES:, :]], axis=0)

    a = conv(wa_ref, cwa_ref, cba_ref)
    val = conv(wv_ref, cwv_ref, cbv_ref)
    o_ref[...] = (jax.nn.gelu(a) * val).astype(o_ref.dtype)


def _ffn_up(xb, seq, w_up, conv_w, conv_b, tm=512, tn=1408):
    n, d = xb.shape
    d_ff = w_up.shape[1] // 2
    tm = min(tm, seq)
    nf = d_ff // tn
    assert nf * tn == d_ff and tn % LANES == 0
    kernel = functools.partial(_ffn_up_kernel, tiles_per_seq=seq // tm)
    halo = tm // HALO
    return pl.pallas_call(
        kernel,
        grid=(nf, n // tm),
        in_specs=[
            pl.BlockSpec((tm, d), lambda j, i: (i, 0)),
            pl.BlockSpec((HALO, d), lambda j, i: (jnp.maximum(i * halo - 1, 0), 0)),
            pl.BlockSpec((d, tn), lambda j, i: (0, j)),
            pl.BlockSpec((d, tn), lambda j, i: (0, nf + j)),
            pl.BlockSpec((CONV_WIDTH, tn), lambda j, i: (0, j)),
            pl.BlockSpec((CONV_WIDTH, tn), lambda j, i: (0, nf + j)),
            pl.BlockSpec((1, tn), lambda j, i: (0, j)),
            pl.BlockSpec((1, tn), lambda j, i: (0, nf + j)),
        ],
        out_specs=pl.BlockSpec((tm, tn), lambda j, i: (i, j)),
        out_shape=jax.ShapeDtypeStruct((n, d_ff), BF16),
        compiler_params=_cparams(("arbitrary", "arbitrary")),
        name="ffn_up",
    )(xb, xb, w_up, w_up, conv_w, conv_w, conv_b.reshape(1, -1), conv_b.reshape(1, -1))


def _ffn_down_kernel(a_ref, w_ref, x_ref, g_ref, b_ref, o_ref, o16_ref):
    z = DEEPNORM_ALPHA * x_ref[...] + _dot(a_ref[...], w_ref[...])
    y = _layer_norm(z, g_ref[...], b_ref[...])
    o_ref[...] = y
    o16_ref[...] = y.astype(BF16)


def _ffn_down(act, w_down, x2d, ln_g, ln_b, tm=256):
    n, d = x2d.shape
    d_ff = act.shape[1]
    tm = min(tm, n)
    row = lambda i: (i, 0)
    const2 = lambda i: (0, 0)
    return pl.pallas_call(
        _ffn_down_kernel,
        grid=(n // tm,),
        in_specs=[
            pl.BlockSpec((tm, d_ff), row),
            pl.BlockSpec((d_ff, d), const2, pipeline_mode=pl.Buffered(1)),
            pl.BlockSpec((tm, d), row),
            pl.BlockSpec((1, d), const2),
            pl.BlockSpec((1, d), const2),
        ],
        out_specs=[pl.BlockSpec((tm, d), row), pl.BlockSpec((tm, d), row)],
        out_shape=[jax.ShapeDtypeStruct((n, d), F32), jax.ShapeDtypeStruct((n, d), BF16)],
        compiler_params=_cparams(("arbitrary",)),
        name="ffn_down",
    )(act, w_down, x2d, ln_g.reshape(1, -1), ln_b.reshape(1, -1))


def _mixer_weight(w_in_l):
    d = w_in_l.shape[0]
    sizes = (A_Q_RANK, A_KV_RANK, IDX_DIM, IDX_HEADS, SSM_WIDTH, 512, 512, 512)
    offs = np.cumsum((0,) + sizes)
    seg = [w_in_l[:, offs[i]:offs[i + 1]] for i in range(len(sizes))]
    pad = lambda w: jnp.zeros((d, w), w_in_l.dtype)
    cols = [seg[0], pad(CKV_OFF - A_Q_RANK), seg[1], seg[2], seg[3], pad(U_OFF - KW_OFF - IDX_DIM - IDX_HEADS),
            seg[4], seg[5], seg[6], seg[7]]
    w = jnp.concatenate(cols, axis=1)
    assert w.shape[1] == PROJ_W
    return w.astype(BF16), w_in_l[:, offs[-1]:].astype(BF16)


def kernel(x, rel_bias, w_in, cq_gain, ckv_gain, w_uq, w_uk, w_uv, w_qidx, lam_re, lam_im, log_step, b_re, b_im, c_re, c_im, d_skip, w_glu, b_glu, w_branch, w_out, ln1_g, ln1_b, w_up, conv_w, conv_b, w_down, ln2_g, ln2_b):
    bsz, seq, d = x.shape
    assert seq % MOBA_BLOCK == 0 and seq % DSA_KC == 0 and seq % S5_L == 0
    n = bsz * seq

    tab_t = jnp.transpose(rel_bias)
    dsa_bkt = jnp.asarray(_toeplitz_buckets(DSA_QB, DSA_KC, (0, DSA_QB, 2 * DSA_QB)))
    moba_bkt = jnp.asarray(_toeplitz_buckets(MOBA_BLOCK, MOBA_BLOCK, (0, MOBA_BLOCK)))
    dsa_bias = _bias_expand(dsa_bkt, tab_t[:A_HEADS]).reshape(A_HEADS, 4, DSA_QB, DSA_KC)
    dsa_bias = jnp.transpose(dsa_bias, (1, 0, 2, 3))
    moba_bias = _bias_expand(moba_bkt, tab_t[A_HEADS:]).reshape(C_HEADS, 3, MOBA_BLOCK, MOBA_BLOCK)
    moba_bias = jnp.transpose(moba_bias, (1, 0, 2, 3))

    x2d = x.reshape(n, d)
    xb = x2d.astype(BF16)
    for l in range(DEPTH):
        w_mix, w_gate = _mixer_weight(w_in[l])
        hm = _proj(xb, w_mix).reshape(bsz, seq, PROJ_W)
        o_a = _dsa(hm, cq_gain[l], ckv_gain[l], w_uq[l], w_uk[l], w_uv[l], w_qidx[l], dsa_bias)
        tab, bb_re, bb_im = _s5_prep(lam_re[l], lam_im[l], log_step[l], b_re[l], b_im[l])
        o_b = _s5(hm, tab, bb_re, bb_im, c_re[l], c_im[l], d_skip[l], w_glu[l], b_glu[l])
        o_c = _moba(hm, moba_bias)
        bw = o_a.shape[-1]
        x2d, xb = _merge(xb, x2d, o_a.reshape(n, bw), o_b.reshape(n, bw), o_c.reshape(n, bw), w_gate,
                         w_branch[l].astype(BF16), w_out[l].astype(BF16), ln1_g[l], ln1_b[l])
        act = _ffn_up(xb, seq, w_up[l].astype(BF16), conv_w[l], conv_b[l])
        x2d, xb = _ffn_down(act, w_down[l].astype(BF16), x2d, ln2_g[l], ln2_b[l])
    return x2d.reshape(bsz, seq, d)
```

```python
import functools
import math

import numpy as np
import jax
import jax.numpy as jnp
from jax import lax
from jax.experimental import pallas as pl
from jax.experimental.pallas import tpu as pltpu

F32 = jnp.float32
BF16 = jnp.bfloat16
I32 = jnp.int32

DEPTH = 2
A_HEADS = 8
A_HEAD_DIM = 64
A_Q_RANK = 384
A_KV_RANK = 256
IDX_HEADS = 16
IDX_DIM = 64
DSA_TOPK = 256
SSM_GROUP = 16
SSM_GROUPS = 32
SSM_WIDTH = SSM_GROUP * SSM_GROUPS
SSM_STATE = 64
C_HEADS = 8
C_HEAD_DIM = 64
MOBA_BLOCK = 256
MOBA_TOPK = 3
N_BRANCH = 3
BRANCH_WIDTH = 512
CONV_WIDTH = 3
REL_BUCKETS = 32
REL_MAX_DIST = 128
LN_EPS = 1e-5
NEG_INF = -1e30
DEEPNORM_ALPHA = (2 * DEPTH) ** 0.25

LANES = 128
SUBLANES = 8
VMEM_LIMIT = 56 * 1024 * 1024

CQ_OFF, CKV_OFF, KW_OFF, U_OFF, Q_OFF, K_OFF, V_OFF, PROJ_W = 0, 512, 768, 1024, 1536, 2048, 2560, 3072

DSA_QB = 128
DSA_KC = 256
S5_LT = 64
NS = SSM_GROUPS * SSM_STATE


def _cparams(sem):
    return pltpu.CompilerParams(dimension_semantics=sem, vmem_limit_bytes=VMEM_LIMIT)


def _dot(a, b):
    return jnp.dot(a, b, preferred_element_type=F32)


def _dot_nt(a, b):
    return lax.dot_general(a, b, (((1,), (1,)), ((), ())), preferred_element_type=F32)


def _loop2(n, body):
    def pair(i, carry):
        body(2 * i)
        body(2 * i + 1)
        return carry

    lax.fori_loop(0, lax.shift_right_logical(n, 1), pair, 0)

    @pl.when((n & 1) == 1)
    def _():
        body(n - 1)


def _order_key(s):
    bits = lax.bitcast_convert_type(s, I32)
    return bits ^ ((bits >> 31) & 0x7FFFFFFF)


def _layer_norm(z, g, b):
    mu = jnp.mean(z, axis=-1, keepdims=True)
    zc = z - mu
    var = jnp.mean(zc * zc, axis=-1, keepdims=True)
    return zc * lax.rsqrt(var + LN_EPS) * g + b


def _proj_kernel(x_ref, w_ref, o_ref):
    o_ref[...] = _dot(x_ref[...], w_ref[...])


def _proj(xb, w, tm=1024, tn=512):
    n, d = xb.shape
    pw = w.shape[1]
    tm = min(tm, n)
    return pl.pallas_call(
        _proj_kernel,
        grid=(n // tm, pw // tn),
        in_specs=[pl.BlockSpec((tm, d), lambda i, j: (i, 0)),
                  pl.BlockSpec((d, tn), lambda i, j: (0, j))],
        out_specs=pl.BlockSpec((tm, tn), lambda i, j: (i, j)),
        out_shape=jax.ShapeDtypeStruct((n, pw), F32),
        compiler_params=_cparams(("arbitrary", "arbitrary")),
        name="proj",
    )(xb, w)


def _bias_kernel(bkt_ref, tab_ref, o_ref):
    b = bkt_ref[...]
    acc = jnp.zeros(o_ref.shape, F32)
    for k in range(REL_BUCKETS):
        acc = jnp.where(b == k, tab_ref[:, k:k + 1], acc)
    o_ref[...] = acc


def _bias_expand(bucket, tab_t, pc=8192):
    p = bucket.shape[1]
    h = tab_t.shape[0]
    return pl.pallas_call(
        _bias_kernel,
        grid=(p // pc,),
        in_specs=[pl.BlockSpec((1, pc), lambda i: (0, i)),
                  pl.BlockSpec((h, REL_BUCKETS), lambda i: (0, 0))],
        out_specs=pl.BlockSpec((h, pc), lambda i: (0, i)),
        out_shape=jax.ShapeDtypeStruct((h, p), F32),
        compiler_params=_cparams(("arbitrary",)),
        name="bias_expand",
    )(bucket, tab_t)


def _t5_bucket_np(dist):
    n = np.maximum(dist, 0)
    exact = REL_BUCKETS // 2
    log_ratio = np.log(np.maximum(n, 1).astype(np.float32) / exact) / math.log(REL_MAX_DIST / exact)
    large = np.minimum(exact + (log_ratio * (REL_BUCKETS - exact)).astype(np.int32), REL_BUCKETS - 1)
    return np.where(n < exact, n, large).astype(np.int32)


def _toeplitz_buckets(rows, cols, deltas):
    i = np.arange(rows)[:, None]
    j = np.arange(cols)[None, :]
    tiles = [_t5_bucket_np(d + i - j) for d in deltas]
    tiles.append(np.full((rows, cols), REL_BUCKETS - 1, np.int32))
    return np.stack(tiles).reshape(1, -1)


def _dsa_kernel(cq_ref, ckv_ref, kw_ref, cqg_ref, ckvg_ref, wuq_ref, wuk_ref, wqi_ref, wuv_ref, bias_ref,
                o_ref, kvn_ref, kk_ref, key_ref, key_t_ref, ql_ref, lg_ref, mx_ref, ls_ref, p_ref, acc_ref, *, seq, n_top):
    qb = pl.program_id(1)
    q0 = qb * DSA_QB
    n_chunk_rows = seq // DSA_KC

    @pl.when(qb == 0)
    def _():
        lane = lax.broadcasted_iota(I32, (DSA_KC, LANES), 1)

        def prep(c, carry):
            rows = pl.ds(pl.multiple_of(c * DSA_KC, DSA_KC), DSA_KC)
            ckv = ckv_ref[rows, :]
            ms = jnp.mean(ckv * ckv, axis=-1, keepdims=True)
            kvn_ref[rows, :] = (ckv * lax.rsqrt(ms + LN_EPS) * ckvg_ref[...]).astype(BF16)
            kw = kw_ref[rows, :]
            kk_ref[rows, :] = jnp.where(lane < IDX_DIM, kw, pltpu.roll(kw, IDX_DIM, 1)).astype(BF16)
            return carry

        lax.fori_loop(0, n_chunk_rows, prep, 0)

    nc = (q0 + DSA_QB + DSA_KC - 1) // DSA_KC

    cq = cq_ref[...]
    ms = jnp.mean(cq * cq, axis=-1, keepdims=True)
    cqn = (cq * lax.rsqrt(ms + LN_EPS) * cqg_ref[...]).astype(BF16)

    lane_q = lax.broadcasted_iota(I32, (DSA_QB, LANES), 1)
    lo_half = lane_q < IDX_DIM
    row_i = lax.broadcasted_iota(I32, (DSA_QB, DSA_KC), 0)
    col_j = lax.broadcasted_iota(I32, (DSA_QB, DSA_KC), 1)
    qpos = q0 + row_i

    qi = _dot(cqn, wqi_ref[...]) * (IDX_DIM ** -0.5)
    wq = kw_ref[pl.ds(pl.multiple_of(q0, DSA_QB), DSA_QB), :] * (IDX_HEADS ** -0.5)
    q_heads = []
    for h in range(IDX_HEADS):
        qp = qi[:, (h // 2) * LANES:(h // 2 + 1) * LANES]
        keep = lo_half if h % 2 == 0 else jnp.logical_not(lo_half)
        q_heads.append(jnp.where(keep, qp, 0.0).astype(BF16))

    def score_body(c, carry):
        kc = kk_ref[pl.ds(pl.multiple_of(c * DSA_KC, DSA_KC), DSA_KC), :]
        s = jnp.zeros((DSA_QB, DSA_KC), F32)
        for h in range(IDX_HEADS):
            r = _dot_nt(q_heads[h], kc)
            s = s + wq[:, IDX_DIM + h:IDX_DIM + h + 1] * jnp.maximum(r, 0.0)
        s = jnp.where(qpos >= c * DSA_KC + col_j, s, -jnp.inf)
        key_ref[c] = _order_key(s)
        key_t_ref[c] = _order_key(s.T).reshape(DSA_KC // SUBLANES, SUBLANES, DSA_QB)
        return carry

    lax.fori_loop(0, nc, score_body, 0)

    n_grp = DSA_KC // SUBLANES
    kpos_t = (lax.broadcasted_iota(I32, (n_grp, SUBLANES, DSA_QB), 0) * SUBLANES
              + lax.broadcasted_iota(I32, (n_grp, SUBLANES, DSA_QB), 1))

    def count(pred):
        def chunk(c):
            one = jnp.where(pred(key_t_ref[c], c), 1.0, 0.0)
            parts = [one[g] for g in range(n_grp)]
            while len(parts) > 1:
                parts = [parts[i] + parts[i + 1] for i in range(0, len(parts), 2)]
            return parts[0]

        tot = lax.fori_loop(0, nc, lambda c, tot: tot + chunk(c), jnp.zeros((SUBLANES, DSA_QB), F32))
        for d in (4, 2, 1):
            tot = tot + pltpu.roll(tot, d, 0)
        return tot

    int_min = jnp.int32(-2 ** 31)
    k_top = float(n_top)
    cnt = count(lambda k, c: k >= 0)
    thr0 = jnp.where(cnt >= k_top, jnp.int32(0), int_min)

    def bit_body(i, thr):
        cand = thr | jnp.left_shift(jnp.int32(1), 30 - i)
        cnt = count(lambda k, c: k >= cand)
        return jnp.where(cnt >= k_top, cand, thr)

    thr_t = lax.fori_loop(0, 31, bit_body, thr0)

    need = k_top - count(lambda k, c: k > thr_t)
    n_eq = count(lambda k, c: k == thr_t)
    any_tie = jnp.max(jnp.where(n_eq > need, 1.0, 0.0)) > 0.0
    idx_bits = int(math.log2(seq))

    def tie_cut():
        def body(i, v):
            cand = v | jnp.left_shift(jnp.int32(1), idx_bits - 1 - i)
            cnt = count(lambda k, c: (k == thr_t) & (c * DSA_KC + kpos_t < cand))
            return jnp.where(cnt < need, cand, v)

        return lax.fori_loop(0, idx_bits, body, jnp.zeros((SUBLANES, DSA_QB), I32))

    cut_t = lax.cond(any_tie, tie_cut, lambda: jnp.full((SUBLANES, DSA_QB), seq, I32))

    def to_rows(v_t):
        def col(x):
            return jnp.broadcast_to(x.astype(F32), (DSA_QB, DSA_QB)).T.astype(I32)

        v = v_t[0:1, :]
        r = (col(v >> 16) << 16) | col(v & 0xFFFF)
        return jnp.concatenate([r] * (DSA_KC // DSA_QB), axis=1)

    thr = to_rows(thr_t)
    cut = to_rows(cut_t)

    q = _dot(cqn, wuq_ref[...])
    for h in range(A_HEADS):
        qp = q[:, (h // 2) * LANES:(h // 2 + 1) * LANES]
        keep = lo_half if h % 2 == 0 else jnp.logical_not(lo_half)
        qm = jnp.where(keep, qp, 0.0).astype(BF16)
        ql_ref[h * DSA_QB:(h + 1) * DSA_QB, :] = (_dot(qm, wuk_ref[h // 2]) * (A_HEAD_DIM ** -0.5)).astype(BF16)
    mx_ref[...] = jnp.full(mx_ref.shape, NEG_INF, F32)
    ls_ref[...] = jnp.zeros(ls_ref.shape, F32)
    acc_ref[...] = jnp.zeros(acc_ref.shape, F32)

    def logit_body(c, carry):
        kv = kvn_ref[pl.ds(pl.multiple_of(c * DSA_KC, DSA_KC), DSA_KC), :]
        k = key_ref[c]
        kpos = c * DSA_KC + col_j
        sel = ((k > thr) | ((k == thr) & (kpos <= cut))) & (qpos >= kpos)
        add_mask = jnp.where(sel, 0.0, NEG_INF)
        t = jnp.minimum((q0 - c * DSA_KC) // DSA_QB, 3)
        s_all = _dot_nt(ql_ref[...], kv)
        for h in range(A_HEADS):
            rows = slice(h * DSA_QB, (h + 1) * DSA_QB)
            sm = s_all[rows, :] + bias_ref[t, h] + add_mask
            lg_ref[c, rows, :] = sm
            mx_ref[h] = jnp.maximum(mx_ref[h], jnp.maximum(sm[:, :LANES], sm[:, LANES:]))
        return carry

    _loop2(nc, lambda c: logit_body(c, 0))

    for h in range(A_HEADS):
        mx_ref[h] = jnp.broadcast_to(jnp.max(mx_ref[h], axis=1, keepdims=True), (DSA_QB, LANES))

    def pv_body(c, carry):
        kv = kvn_ref[pl.ds(pl.multiple_of(c * DSA_KC, DSA_KC), DSA_KC), :]
        for h in range(A_HEADS):
            rows = slice(h * DSA_QB, (h + 1) * DSA_QB)
            m = mx_ref[h]
            p_lo = jnp.exp(lg_ref[c, rows, :LANES] - m)
            p_hi = jnp.exp(lg_ref[c, rows, LANES:] - m)
            ls_ref[h] += p_lo + p_hi
            p_ref[rows, :LANES] = p_lo.astype(BF16)
            p_ref[rows, LANES:] = p_hi.astype(BF16)
        acc_ref[...] += _dot(p_ref[...], kv)
        return carry

    _loop2(nc, lambda c: pv_body(c, 0))

    for pr in range(A_HEADS // 2):
        o = jnp.zeros((DSA_QB, LANES), F32)
        for h in (2 * pr, 2 * pr + 1):
            l = jnp.sum(ls_ref[h], axis=1, keepdims=True)
            o_lat = (acc_ref[h * DSA_QB:(h + 1) * DSA_QB, :] / l).astype(BF16)
            o = o + _dot(o_lat, wuv_ref[h])
        o_ref[:, pr * LANES:(pr + 1) * LANES] = o.astype(o_ref.dtype)


def _dsa(hm, cq_gain, ckv_gain, w_uq, w_uk, w_uv, w_qidx, bias_tiles):
    bsz, seq, _ = hm.shape
    n_top = min(DSA_TOPK, seq // 4)
    hd = A_HEADS * A_HEAD_DIM
    wuq = w_uq.reshape(A_Q_RANK, hd).astype(BF16)
    wuk = jnp.transpose(w_uk, (1, 2, 0)).reshape(A_HEADS // 2, 2 * A_HEAD_DIM, A_KV_RANK).astype(BF16)
    wqi = w_qidx.reshape(A_Q_RANK, IDX_HEADS * IDX_DIM).astype(BF16)
    wv = jnp.transpose(w_uv, (1, 0, 2))
    zeros = jnp.zeros_like(wv)
    even = (jnp.arange(A_HEADS) % 2 == 0)[:, None, None]
    wuv = jnp.concatenate([jnp.where(even, wv, zeros), jnp.where(even, zeros, wv)], axis=-1).astype(BF16)
    kernel = functools.partial(_dsa_kernel, seq=seq, n_top=n_top)
    const2 = lambda b, q: (0, 0)
    const3 = lambda b, q: (0, 0, 0)
    return pl.pallas_call(
        kernel,
        grid=(bsz, seq // DSA_QB),
        in_specs=[
            pl.BlockSpec((None, DSA_QB, A_Q_RANK), lambda b, q: (b, q, CQ_OFF // A_Q_RANK)),
            pl.BlockSpec((None, seq, A_KV_RANK), lambda b, q: (b, 0, CKV_OFF // A_KV_RANK)),
            pl.BlockSpec((None, seq, LANES), lambda b, q: (b, 0, KW_OFF // LANES)),
            pl.BlockSpec((1, A_Q_RANK), const2),
            pl.BlockSpec((1, A_KV_RANK), const2),
            pl.BlockSpec(wuq.shape, const2),
            pl.BlockSpec(wuk.shape, const3),
            pl.BlockSpec(wqi.shape, const2),
            pl.BlockSpec(wuv.shape, const3),
            pl.BlockSpec(bias_tiles.shape, lambda b, q: (0, 0, 0, 0)),
        ],
        out_specs=pl.BlockSpec((None, DSA_QB, hd), lambda b, q: (b, q, 0)),
        out_shape=jax.ShapeDtypeStruct((bsz, seq, hd), BF16),
        scratch_shapes=[
            pltpu.VMEM((seq, A_KV_RANK), BF16),
            pltpu.VMEM((seq, LANES), BF16),
            pltpu.VMEM((seq // DSA_KC, DSA_QB, DSA_KC), I32),
            pltpu.VMEM((seq // DSA_KC, DSA_KC // SUBLANES, SUBLANES, DSA_QB), I32),
            pltpu.VMEM((A_HEADS * DSA_QB, A_KV_RANK), BF16),
            pltpu.VMEM((seq // DSA_KC, A_HEADS * DSA_QB, DSA_KC), F32),
            pltpu.VMEM((A_HEADS, DSA_QB, LANES), F32),
            pltpu.VMEM((A_HEADS, DSA_QB, LANES), F32),
            pltpu.VMEM((A_HEADS * DSA_QB, DSA_KC), BF16),
            pltpu.VMEM((A_HEADS * DSA_QB, A_KV_RANK), F32),
        ],
        compiler_params=_cparams(("arbitrary", "arbitrary")),
        name="dsa",
    )(hm, hm, hm, cq_gain.reshape(1, -1), ckv_gain.reshape(1, -1), wuq, wuk, wqi, wuv, bias_tiles)


S5_POWERS = (1,)
S5_ROWS = 2 * len(S5_POWERS) + 2


def _s5_prep_kernel(lre_ref, lim_ref, ls_ref, bre_ref, bim_ref, tab_ref, bbre_ref, bbim_ref):
    lre = lre_ref[...]
    lim = lim_ref[...]
    step = jnp.exp(ls_ref[...])
    for r, k in enumerate(S5_POWERS):
        mag = jnp.exp(k * lre * step)
        ang = k * lim * step
        tab_ref[2 * r:2 * r + 1, :] = mag * jnp.cos(ang)
        tab_ref[2 * r + 1:2 * r + 2, :] = mag * jnp.sin(ang)
    a = tab_ref[0:1, :] - 1.0
    b = tab_ref[1:2, :]
    den = lre * lre + lim * lim
    f_re = (a * lre + b * lim) / den
    f_im = (b * lre - a * lim) / den
    base = 2 * len(S5_POWERS)
    tab_ref[base:base + 1, :] = f_re
    tab_ref[base + 1:base + 2, :] = f_im
    bre = bre_ref[...]
    bim = bim_ref[...]
    bbre_ref[...] = f_re * bre - f_im * bim
    bbim_ref[...] = f_re * bim + f_im * bre


def _s5_prep(lam_re, lam_im, log_step, b_re, b_im):
    lre = lam_re.reshape(1, NS)
    lim = lam_im.reshape(1, NS)
    ls = jnp.repeat(log_step, SSM_STATE).reshape(1, NS)
    bre = jnp.transpose(b_re, (2, 0, 1)).reshape(SSM_GROUP, NS)
    bim = jnp.transpose(b_im, (2, 0, 1)).reshape(SSM_GROUP, NS)
    return pl.pallas_call(
        _s5_prep_kernel,
        out_shape=(jax.ShapeDtypeStruct((S5_ROWS, NS), F32),
                   jax.ShapeDtypeStruct((SSM_GROUP, NS), F32),
                   jax.ShapeDtypeStruct((SSM_GROUP, NS), F32)),
        name="s5_prep",
    )(lre, lim, ls, bre, bim)


def _s5_kernel(u_ref, perm_ref, perm_t_ref, bm_ref, cm_ref, lam_ref, dsk_ref, wg_ref, bg_ref, o_ref, xs_ref, c_ref,
               *, lane_chunk, unroll):
    tc = pl.program_id(1)
    rows_all = SUBLANES * S5_LT

    @pl.when(tc == 0)
    def _():
        c_ref[...] = jnp.zeros(c_ref.shape, F32)

    u = u_ref[...].reshape(rows_all, SSM_WIDTH)
    u_tm = _dot(perm_ref[...], u.astype(BF16)).astype(BF16)
    n_sg = SSM_WIDTH // LANES
    sg_w = NS // n_sg
    for sg in range(n_sg):
        bu = _dot(u_tm[:, sg * LANES:(sg + 1) * LANES], bm_ref[sg])
        xs_ref[:, sg * sg_w:(sg + 1) * sg_w] = bu[:, :sg_w]
        xs_ref[:, NS + sg * sg_w:NS + (sg + 1) * sg_w] = bu[:, sg_w:]

    for lc in range(NS // lane_chunk):
        re_cols = slice(lc * lane_chunk, (lc + 1) * lane_chunk)
        im_cols = slice(NS + lc * lane_chunk, NS + (lc + 1) * lane_chunk)
        l_re = lam_ref[0, :, re_cols]
        l_im = lam_ref[1, :, re_cols]

        def step_body(i, carry, re_cols=re_cols, im_cols=im_cols, l_re=l_re, l_im=l_im):
            x_re, x_im = carry
            for k in range(unroll):
                rows = pl.ds(pl.multiple_of((i * unroll + k) * SUBLANES, SUBLANES), SUBLANES)
                x_re, x_im = (l_re * x_re - l_im * x_im + xs_ref[rows, re_cols],
                              l_re * x_im + l_im * x_re + xs_ref[rows, im_cols])
                xs_ref[rows, re_cols] = x_re
                xs_ref[rows, im_cols] = x_im
            return x_re, x_im

        x_re, x_im = lax.fori_loop(0, S5_LT // unroll, step_body, (c_ref[:, re_cols], c_ref[:, im_cols]))
        c_ref[:, re_cols] = x_re
        c_ref[:, im_cols] = x_im

    y_tm = jnp.concatenate(
        [_dot(xs_ref[:, sg * sg_w:(sg + 1) * sg_w].astype(BF16), cm_ref[sg, :sg_w, :])
         + _dot(xs_ref[:, NS + sg * sg_w:NS + (sg + 1) * sg_w].astype(BF16), cm_ref[sg, sg_w:, :])
         for sg in range(n_sg)], axis=1)
    y1 = y_tm.astype(BF16)
    r1 = y_tm - y1.astype(F32)
    y2 = r1.astype(BF16)
    y3 = (r1 - y2.astype(F32)).astype(BF16)
    pt = perm_t_ref[...]
    y = _dot(pt, y1) + (_dot(pt, y2) + _dot(pt, y3)) + dsk_ref[...] * u
    y = jax.nn.gelu(y)
    z = _dot(y.astype(BF16), wg_ref[...]) + bg_ref[...]
    o_ref[...] = (y * jax.nn.sigmoid(z)).astype(o_ref.dtype).reshape(o_ref.shape)


def _s5(hm, tab, bb_re, bb_im, c_re, c_im, d_skip, w_glu, b_glu, lane_chunk=1024):
    bsz, seq, _ = hm.shape
    eye = jnp.eye(SSM_GROUPS, dtype=F32)

    def in_map(bb):
        bb = jnp.transpose(bb.reshape(SSM_GROUP, SSM_GROUPS, SSM_STATE), (1, 0, 2))
        return jnp.einsum('gpn,gh->gphn', bb, eye).reshape(SSM_WIDTH, NS)

    def out_map(c):
        return jnp.einsum('gpn,gh->gnhp', c, eye).reshape(NS, SSM_WIDTH)

    n_sg = SSM_WIDTH // LANES
    sg_w = NS // n_sg

    def diag(m, rows, cols):
        return jnp.stack([m[s * rows:(s + 1) * rows, s * cols:(s + 1) * cols] for s in range(n_sg)])

    bm = jnp.concatenate([diag(in_map(bb_re), LANES, sg_w), diag(in_map(bb_im), LANES, sg_w)], axis=2).astype(BF16)
    cm = jnp.concatenate([diag(out_map(c_re), sg_w, LANES), -diag(out_map(c_im), sg_w, LANES)], axis=1).astype(BF16)
    lam = jnp.broadcast_to(tab[0:2, None, :], (2, SUBLANES, NS))
    rows_all = SUBLANES * S5_LT
    r = np.arange(rows_all)
    perm = np.zeros((rows_all, rows_all), np.float32)
    perm[r, (r % SUBLANES) * S5_LT + r // SUBLANES] = 1.0
    perm_t = jnp.asarray(perm.T, BF16)
    perm = jnp.asarray(perm, BF16)
    assert bsz % SUBLANES == 0 and seq % S5_LT == 0
    kernel = functools.partial(_s5_kernel, lane_chunk=lane_chunk, unroll=4)
    const2 = lambda b, t: (0, 0)
    once = dict(pipeline_mode=pl.Buffered(1))
    return pl.pallas_call(
        kernel,
        grid=(bsz // SUBLANES, seq // S5_LT),
        in_specs=[
            pl.BlockSpec((SUBLANES, S5_LT, SSM_WIDTH), lambda b, t: (b, t, U_OFF // SSM_WIDTH)),
            pl.BlockSpec(perm.shape, const2, **once),
            pl.BlockSpec(perm.shape, const2, **once),
            pl.BlockSpec(bm.shape, lambda b, t: (0, 0, 0), **once),
            pl.BlockSpec(cm.shape, lambda b, t: (0, 0, 0), **once),
            pl.BlockSpec(lam.shape, lambda b, t: (0, 0, 0), **once),
            pl.BlockSpec((1, SSM_WIDTH), const2),
            pl.BlockSpec((SSM_WIDTH, SSM_WIDTH), const2),
            pl.BlockSpec((1, SSM_WIDTH), const2),
        ],
        out_specs=pl.BlockSpec((SUBLANES, S5_LT, SSM_WIDTH), lambda b, t: (b, t, 0)),
        out_shape=jax.ShapeDtypeStruct((bsz, seq, SSM_WIDTH), BF16),
        scratch_shapes=[pltpu.VMEM((rows_all, 2 * NS), F32),
                        pltpu.VMEM((SUBLANES, 2 * NS), F32)],
        compiler_params=_cparams(("arbitrary", "arbitrary")),
        name="s5",
    )(hm, perm, perm_t, bm, cm, lam, d_skip.reshape(1, -1), w_glu.astype(BF16), b_glu.reshape(1, -1))


def _moba_kernel(q_ref, k_ref, v_ref, bias_ref, o_ref, kb_ref, vb_ref, km_ref, lg_ref, mx_ref, ls_ref, p_ref, acc_ref,
                 *, n_blk):
    qb = pl.program_id(2)
    blk = MOBA_BLOCK

    @pl.when(qb == 0)
    def _():
        km_ref[...] = jnp.zeros(km_ref.shape, F32)
        lane_k = lax.broadcasted_iota(I32, (blk, LANES), 1)
        for n in range(n_blk):
            rows = slice(n * blk, (n + 1) * blk)
            kblk = k_ref[rows, :]
            km_ref[n:n + 1, :] = jnp.mean(kblk, axis=0, keepdims=True)
            kb_ref[0, rows, :] = jnp.where(lane_k < C_HEAD_DIM, kblk,
                                           jnp.where(lane_k == C_HEAD_DIM + n, 1.0, 0.0)).astype(BF16)
            kb_ref[1, rows, :] = jnp.where(lane_k >= C_HEAD_DIM, kblk, jnp.where(lane_k == n, 1.0, 0.0)).astype(BF16)
            vb_ref[rows, :] = v_ref[rows, :].astype(BF16)

    q = q_ref[...]
    lane = lax.broadcasted_iota(I32, (blk, LANES), 1)
    row_i = lax.broadcasted_iota(I32, (blk, blk), 0)
    col_j = lax.broadcasted_iota(I32, (blk, blk), 1)
    causal_add = jnp.where(row_i >= col_j, 0.0, NEG_INF)
    blk_id = lax.broadcasted_iota(I32, (SUBLANES, blk), 0)
    blk_f = blk_id.astype(F32)
    km = km_ref[...]
    lane_m = lax.broadcasted_iota(I32, (SUBLANES, LANES), 1)
    km_both = jnp.concatenate([jnp.where(lane_m < C_HEAD_DIM, km, 0.0), jnp.where(lane_m >= C_HEAD_DIM, km, 0.0)], axis=0)
    km_hi = km_both.astype(BF16)
    km_lo = (km_both - km_hi.astype(F32)).astype(BF16)
    q_hi = q.astype(BF16)
    q_lo = (q - q_hi.astype(F32)).astype(BF16)
    gate_both = _dot_nt(km_hi, q_hi) + (_dot_nt(km_hi, q_lo) + _dot_nt(km_lo, q_hi))
    scale = C_HEAD_DIM ** -0.5
    own_rows = pl.ds(pl.multiple_of(qb * blk, blk), blk)

    def over_blocks(x, op):
        for d in (4, 2, 1):
            x = op(x, pltpu.roll(x, d, 0))
        return x

    qa = []
    for s in range(2):
        keep = (lane < C_HEAD_DIM) if s == 0 else (lane >= C_HEAD_DIM)
        gate = gate_both[s * SUBLANES:(s + 1) * SUBLANES, :]
        g = jnp.where(blk_id < qb, gate, -jnp.inf)
        sel = jnp.zeros((SUBLANES, blk), F32)
        for _ in range(MOBA_TOPK):
            mx = over_blocks(g, jnp.maximum)
            is_max = (g == mx) & (g > -jnp.inf)
            first = over_blocks(jnp.where(is_max, blk_f, float(SUBLANES)), jnp.minimum)
            pick = blk_f == first
            sel = jnp.where(pick, 1.0, sel)
            g = jnp.where(pick, -jnp.inf, g)
        add_t = jnp.where((sel > 0.5) | (blk_id == qb), 0.0, NEG_INF)
        lead = C_HEAD_DIM if s == 0 else 0
        pieces = ([jnp.zeros((lead, blk), F32)] if lead else []) + [add_t]
        pieces.append(jnp.zeros((LANES - lead - SUBLANES, blk), F32))
        add_q = jnp.concatenate(pieces, axis=0).T
        qa.append(jnp.where(keep, q * scale, add_q).astype(BF16))
        sm = _dot_nt(qa[s], kb_ref[s, own_rows, :]) + bias_ref[0, s] + causal_add
        lg_ref[s, qb] = sm
        mx_ref[s] = jnp.maximum(sm[:, :LANES], sm[:, LANES:])
        ls_ref[s] = jnp.zeros((blk, LANES), F32)
        acc_ref[s] = jnp.zeros((blk, LANES), F32)

    def logit_body(n, carry):
        rows = pl.ds(pl.multiple_of(n * blk, blk), blk)
        t = jnp.minimum(qb - n, 2)
        for s in range(2):
            sm = _dot_nt(qa[s], kb_ref[s, rows, :]) + bias_ref[t, s]
            lg_ref[s, n] = sm
            mx_ref[s] = jnp.maximum(mx_ref[s], jnp.maximum(sm[:, :LANES], sm[:, LANES:]))
        return carry

    _loop2(qb, lambda n: logit_body(n, 0))

    for s in range(2):
        mx_ref[s] = jnp.broadcast_to(jnp.max(mx_ref[s], axis=1, keepdims=True), (blk, LANES))

    def pv_body(n, carry):
        vc = vb_ref[pl.ds(pl.multiple_of(n * blk, blk), blk), :]
        for s in range(2):
            m = mx_ref[s]
            p_lo = jnp.exp(lg_ref[s, n, :, :LANES] - m)
            p_hi = jnp.exp(lg_ref[s, n, :, LANES:] - m)
            ls_ref[s] += p_lo + p_hi
            p_ref[s, :, :LANES] = p_lo.astype(BF16)
            p_ref[s, :, LANES:] = p_hi.astype(BF16)
            acc_ref[s] += _dot(p_ref[s], vc)
        return carry

    _loop2(qb + 1, lambda n: pv_body(n, 0))

    o = jnp.where(lane < C_HEAD_DIM,
                  acc_ref[0] / jnp.sum(ls_ref[0], axis=1, keepdims=True),
                  acc_ref[1] / jnp.sum(ls_ref[1], axis=1, keepdims=True))
    o_ref[...] = o.astype(o_ref.dtype)


def _moba(hm, bias_tiles):
    bsz, seq, _ = hm.shape
    n_blk = seq // MOBA_BLOCK
    n_pair = C_HEADS // 2
    assert n_blk <= SUBLANES
    kernel = functools.partial(_moba_kernel, n_blk=n_blk)
    return pl.pallas_call(
        kernel,
        grid=(bsz, n_pair, n_blk),
        in_specs=[
            pl.BlockSpec((None, MOBA_BLOCK, LANES), lambda b, h, q: (b, q, Q_OFF // LANES + h)),
            pl.BlockSpec((None, seq, LANES), lambda b, h, q: (b, 0, K_OFF // LANES + h)),
            pl.BlockSpec((None, seq, LANES), lambda b, h, q: (b, 0, V_OFF // LANES + h)),
            pl.BlockSpec((3, 2, MOBA_BLOCK, MOBA_BLOCK), lambda b, h, q: (0, h, 0, 0)),
        ],
        out_specs=pl.BlockSpec((None, MOBA_BLOCK, LANES), lambda b, h, q: (b, q, h)),
        out_shape=jax.ShapeDtypeStruct((bsz, seq, C_HEADS * C_HEAD_DIM), BF16),
        scratch_shapes=[
            pltpu.VMEM((2, seq, LANES), BF16),
            pltpu.VMEM((seq, LANES), BF16),
            pltpu.VMEM((SUBLANES, LANES), F32),
            pltpu.VMEM((2, n_blk, MOBA_BLOCK, MOBA_BLOCK), F32),
            pltpu.VMEM((2, MOBA_BLOCK, LANES), F32),
            pltpu.VMEM((2, MOBA_BLOCK, LANES), F32),
            pltpu.VMEM((2, MOBA_BLOCK, MOBA_BLOCK), BF16),
            pltpu.VMEM((2, MOBA_BLOCK, LANES), F32),
        ],
        compiler_params=_cparams(("arbitrary", "arbitrary", "arbitrary")),
        name="moba",
    )(hm, hm, hm, bias_tiles)


def _merge_kernel(xb_ref, x_ref, oa_ref, ob_ref, oc_ref, wg0_ref, wg1_ref, wg2_ref, wb_ref, wo_ref, g_ref, b_ref,
                  o_ref, o16_ref, mg_ref):
    j = pl.program_id(1)
    tn = wg0_ref.shape[1]
    xb = xb_ref[...]
    merged = None
    for n, (o_n, wg_n) in enumerate(((oa_ref, wg0_ref), (ob_ref, wg1_ref), (oc_ref, wg2_ref))):
        gate = jax.nn.sigmoid(_dot(xb, wg_n[...]))
        y = _dot(o_n[...], wb_ref[n])
        merged = gate * y if merged is None else merged + gate * y
    mg_ref[:, pl.ds(pl.multiple_of(j * tn, tn), tn)] = merged.astype(BF16)

    @pl.when(j == pl.num_programs(1) - 1)
    def _():
        z = DEEPNORM_ALPHA * x_ref[...] + _dot(mg_ref[...], wo_ref[...])
        y = _layer_norm(z, g_ref[...], b_ref[...])
        o_ref[...] = y
        o16_ref[...] = y.astype(BF16)


def _merge(xb, x2d, oa, ob, oc, wg, wb, wo, ln_g, ln_b, tm=512, tn=256):
    n, d = x2d.shape
    tm = min(tm, n)
    nd = d // tn
    bw = oa.shape[1]
    row = lambda i, j: (i, 0)
    const2 = lambda i, j: (0, 0)
    return pl.pallas_call(
        _merge_kernel,
        grid=(n // tm, nd),
        in_specs=[
            pl.BlockSpec((tm, d), row),
            pl.BlockSpec((tm, d), row),
            pl.BlockSpec((tm, bw), row),
            pl.BlockSpec((tm, bw), row),
            pl.BlockSpec((tm, bw), row),
            pl.BlockSpec((d, tn), lambda i, j: (0, j)),
            pl.BlockSpec((d, tn), lambda i, j: (0, nd + j)),
            pl.BlockSpec((d, tn), lambda i, j: (0, 2 * nd + j)),
            pl.BlockSpec((N_BRANCH, bw, tn), lambda i, j: (0, 0, j)),
            pl.BlockSpec((d, d), const2, pipeline_mode=pl.Buffered(1)),
            pl.BlockSpec((1, d), const2),
            pl.BlockSpec((1, d), const2),
        ],
        out_specs=[pl.BlockSpec((tm, d), row), pl.BlockSpec((tm, d), row)],
        out_shape=[jax.ShapeDtypeStruct((n, d), F32), jax.ShapeDtypeStruct((n, d), BF16)],
        scratch_shapes=[pltpu.VMEM((tm, d), BF16)],
        compiler_params=_cparams(("arbitrary", "arbitrary")),
        name="merge",
    )(xb, x2d, oa, ob, oc, wg, wg, wg, wb, wo, ln_g.reshape(1, -1), ln_b.reshape(1, -1))


def _ffn_up_kernel(x_ref, wa_ref, wv_ref, cwa_ref, cwv_ref, cba_ref, cbv_ref, o_ref, tail_ref, *, tiles_per_seq):
    i = pl.program_id(1)
    tm = x_ref.shape[0]

    @pl.when(i % tiles_per_seq == 0)
    def _():
        tail_ref[...] = jnp.zeros(tail_ref.shape, F32)

    xb = x_ref[...]
    row = lax.broadcasted_iota(I32, (SUBLANES, o_ref.shape[1]), 0)
    last = SUBLANES - 1

    def conv(part, w_ref, cw_ref, cb_ref):
        h = _dot(xb, w_ref[...])
        hh = tail_ref[part]
        tail_ref[part] = h[tm - SUBLANES:tm, :]
        w0, w1, w2, cb = cw_ref[0:1, :], cw_ref[1:2, :], cw_ref[2:3, :], cb_ref[...]
        body = cb + w0 * pltpu.roll(h, 2, 0) + w1 * pltpu.roll(h, 1, 0) + w2 * h
        h8 = h[0:SUBLANES, :]
        p1 = jnp.where(row == 0, hh[last:last + 1, :], pltpu.roll(h8, 1, 0))
        p2 = jnp.where(row == 0, hh[last - 1:last, :], jnp.where(row == 1, hh[last:last + 1, :], pltpu.roll(h8, 2, 0)))
        head = cb + w0 * p2 + w1 * p1 + w2 * h8
        return jnp.concatenate([head, body[SUBLANES:, :]], axis=0)

    a = conv(0, wa_ref, cwa_ref, cba_ref)
    val = conv(1, wv_ref, cwv_ref, cbv_ref)
    o_ref[...] = (jax.nn.gelu(a) * val).astype(o_ref.dtype)


def _ffn_up(xb, seq, w_up, conv_w, conv_b, tm=512, tn=1408):
    n, d = xb.shape
    d_ff = w_up.shape[1] // 2
    tm = min(tm, seq)
    nf = d_ff // tn
    assert nf * tn == d_ff and tn % LANES == 0
    kernel = functools.partial(_ffn_up_kernel, tiles_per_seq=seq // tm)
    return pl.pallas_call(
        kernel,
        grid=(nf, n // tm),
        in_specs=[
            pl.BlockSpec((tm, d), lambda j, i: (i, 0)),
            pl.BlockSpec((d, tn), lambda j, i: (0, j)),
            pl.BlockSpec((d, tn), lambda j, i: (0, nf + j)),
            pl.BlockSpec((CONV_WIDTH, tn), lambda j, i: (0, j)),
            pl.BlockSpec((CONV_WIDTH, tn), lambda j, i: (0, nf + j)),
            pl.BlockSpec((1, tn), lambda j, i: (0, j)),
            pl.BlockSpec((1, tn), lambda j, i: (0, nf + j)),
        ],
        out_specs=pl.BlockSpec((tm, tn), lambda j, i: (i, j)),
        out_shape=jax.ShapeDtypeStruct((n, d_ff), BF16),
        scratch_shapes=[pltpu.VMEM((2, SUBLANES, tn), F32)],
        compiler_params=_cparams(("arbitrary", "arbitrary")),
        name="ffn_up",
    )(xb, w_up, w_up, conv_w, conv_w, conv_b.reshape(1, -1), conv_b.reshape(1, -1))


def _ffn_down_kernel(a_ref, w_ref, x_ref, g_ref, b_ref, o_ref, o16_ref):
    z = DEEPNORM_ALPHA * x_ref[...] + _dot(a_ref[...], w_ref[...])
    y = _layer_norm(z, g_ref[...], b_ref[...])
    o_ref[...] = y
    o16_ref[...] = y.astype(BF16)


def _ffn_down(act, w_down, x2d, ln_g, ln_b, tm=256):
    n, d = x2d.shape
    d_ff = act.shape[1]
    tm = min(tm, n)
    row = lambda i: (i, 0)
    const2 = lambda i: (0, 0)
    return pl.pallas_call(
        _ffn_down_kernel,
        grid=(n // tm,),
        in_specs=[
            pl.BlockSpec((tm, d_ff), row),
            pl.BlockSpec((d_ff, d), const2, pipeline_mode=pl.Buffered(1)),
            pl.BlockSpec((tm, d), row),
            pl.BlockSpec((1, d), const2),
            pl.BlockSpec((1, d), const2),
        ],
        out_specs=[pl.BlockSpec((tm, d), row), pl.BlockSpec((tm, d), row)],
        out_shape=[jax.ShapeDtypeStruct((n, d), F32), jax.ShapeDtypeStruct((n, d), BF16)],
        compiler_params=_cparams(("arbitrary",)),
        name="ffn_down",
    )(act, w_down, x2d, ln_g.reshape(1, -1), ln_b.reshape(1, -1))


def _mixer_weight(w_in_l):
    d = w_in_l.shape[0]
    sizes = (A_Q_RANK, A_KV_RANK, IDX_DIM, IDX_HEADS, SSM_WIDTH, 512, 512, 512)
    offs = np.cumsum((0,) + sizes)
    seg = [w_in_l[:, offs[i]:offs[i + 1]] for i in range(len(sizes))]
    pad = lambda w: jnp.zeros((d, w), w_in_l.dtype)
    cols = [seg[0], pad(CKV_OFF - A_Q_RANK), seg[1], seg[2], seg[3], pad(U_OFF - KW_OFF - IDX_DIM - IDX_HEADS),
            seg[4], seg[5], seg[6], seg[7]]
    w = jnp.concatenate(cols, axis=1)
    assert w.shape[1] == PROJ_W
    return w.astype(BF16), w_in_l[:, offs[-1]:].astype(BF16)


def kernel(x, rel_bias, w_in, cq_gain, ckv_gain, w_uq, w_uk, w_uv, w_qidx, lam_re, lam_im, log_step, b_re, b_im, c_re, c_im, d_skip, w_glu, b_glu, w_branch, w_out, ln1_g, ln1_b, w_up, conv_w, conv_b, w_down, ln2_g, ln2_b):
    bsz, seq, d = x.shape
    assert seq % MOBA_BLOCK == 0 and seq % DSA_KC == 0
    n = bsz * seq

    tab_t = jnp.transpose(rel_bias)
    dsa_bkt = jnp.asarray(_toeplitz_buckets(DSA_QB, DSA_KC, (0, DSA_QB, 2 * DSA_QB)))
    moba_bkt = jnp.asarray(_toeplitz_buckets(MOBA_BLOCK, MOBA_BLOCK, (0, MOBA_BLOCK)))
    dsa_bias = _bias_expand(dsa_bkt, tab_t[:A_HEADS]).reshape(A_HEADS, 4, DSA_QB, DSA_KC)
    dsa_bias = jnp.transpose(dsa_bias, (1, 0, 2, 3))
    moba_bias = _bias_expand(moba_bkt, tab_t[A_HEADS:]).reshape(C_HEADS, 3, MOBA_BLOCK, MOBA_BLOCK)
    moba_bias = jnp.transpose(moba_bias, (1, 0, 2, 3))

    x2d = x.reshape(n, d)
    xb = x2d.astype(BF16)
    for l in range(DEPTH):
        w_mix, w_gate = _mixer_weight(w_in[l])
        hm = _proj(xb, w_mix).reshape(bsz, seq, PROJ_W)
        o_a = _dsa(hm, cq_gain[l], ckv_gain[l], w_uq[l], w_uk[l], w_uv[l], w_qidx[l], dsa_bias)
        tab, bb_re, bb_im = _s5_prep(lam_re[l], lam_im[l], log_step[l], b_re[l], b_im[l])
        o_b = _s5(hm, tab, bb_re, bb_im, c_re[l], c_im[l], d_skip[l], w_glu[l], b_glu[l])
        o_c = _moba(hm, moba_bias)
        bw = o_a.shape[-1]
        x2d, xb = _merge(xb, x2d, o_a.reshape(n, bw), o_b.reshape(n, bw), o_c.reshape(n, bw), w_gate,
                         w_branch[l].astype(BF16), w_out[l].astype(BF16), ln1_g[l], ln1_b[l])
        act = _ffn_up(xb, seq, w_up[l].astype(BF16), conv_w[l], conv_b[l])
        x2d, xb = _ffn_down(act, w_down[l].astype(BF16), x2d, ln2_g[l], ln2_b[l])
    return x2d.reshape(bsz, seq, d)
```

```python
import functools
import math

import numpy as np
import jax
import jax.numpy as jnp
from jax import lax
from jax.experimental import pallas as pl
from jax.experimental.pallas import tpu as pltpu

F32 = jnp.float32
BF16 = jnp.bfloat16
I32 = jnp.int32

DEPTH = 2
A_HEADS = 8
A_HEAD_DIM = 64
A_Q_RANK = 384
A_KV_RANK = 256
IDX_HEADS = 16
IDX_DIM = 64
DSA_TOPK = 256
SSM_GROUP = 16
SSM_GROUPS = 32
SSM_WIDTH = SSM_GROUP * SSM_GROUPS
SSM_STATE = 64
C_HEADS = 8
C_HEAD_DIM = 64
MOBA_BLOCK = 256
MOBA_TOPK = 3
N_BRANCH = 3
BRANCH_WIDTH = 512
CONV_WIDTH = 3
REL_BUCKETS = 32
REL_MAX_DIST = 128
LN_EPS = 1e-5
NEG_INF = -1e30
DEEPNORM_ALPHA = (2 * DEPTH) ** 0.25

LANES = 128
SUBLANES = 8
VMEM_LIMIT = 56 * 1024 * 1024

CQ_OFF, CKV_OFF, KW_OFF, U_OFF, Q_OFF, K_OFF, V_OFF, PROJ_W = 0, 512, 768, 1024, 1536, 2048, 2560, 3072

DSA_QB = 256
DSA_KC = 256
S5_LT = 64
MOBA_PAIRS = 2
NS = SSM_GROUPS * SSM_STATE


def _cparams(sem):
    return pltpu.CompilerParams(dimension_semantics=sem, vmem_limit_bytes=VMEM_LIMIT)


def _dot(a, b):
    return jnp.dot(a, b, preferred_element_type=F32)


def _dot_nt(a, b):
    return lax.dot_general(a, b, (((1,), (1,)), ((), ())), preferred_element_type=F32)


def _loop2(n, body):
    def pair(i, carry):
        body(2 * i)
        body(2 * i + 1)
        return carry

    lax.fori_loop(0, lax.shift_right_logical(n, 1), pair, 0)

    @pl.when((n & 1) == 1)
    def _():
        body(n - 1)


def _order_key(s):
    bits = lax.bitcast_convert_type(s, I32)
    return bits ^ ((bits >> 31) & 0x7FFFFFFF)


def _layer_norm(z, g, b):
    mu = jnp.mean(z, axis=-1, keepdims=True)
    zc = z - mu
    var = jnp.mean(zc * zc, axis=-1, keepdims=True)
    return zc * lax.rsqrt(var + LN_EPS) * g + b


def _proj_kernel(x_ref, w_ref, o_ref):
    o_ref[...] = _dot(x_ref[...], w_ref[...])


def _proj(xb, w, tm=1024, tn=512):
    n, d = xb.shape
    pw = w.shape[1]
    tm = min(tm, n)
    return pl.pallas_call(
        _proj_kernel,
        grid=(n // tm, pw // tn),
        in_specs=[pl.BlockSpec((tm, d), lambda i, j: (i, 0)),
                  pl.BlockSpec((d, tn), lambda i, j: (0, j))],
        out_specs=pl.BlockSpec((tm, tn), lambda i, j: (i, j)),
        out_shape=jax.ShapeDtypeStruct((n, pw), F32),
        compiler_params=_cparams(("arbitrary", "arbitrary")),
        name="proj",
    )(xb, w)


def _bias_kernel(bkt_ref, tab_ref, o_ref):
    b = bkt_ref[...]
    acc = jnp.zeros(o_ref.shape, F32)
    for k in range(REL_BUCKETS):
        acc = jnp.where(b == k, tab_ref[:, k:k + 1], acc)
    o_ref[...] = acc


def _bias_expand(bucket, tab_t, pc=8192):
    p = bucket.shape[1]
    h = tab_t.shape[0]
    return pl.pallas_call(
        _bias_kernel,
        grid=(p // pc,),
        in_specs=[pl.BlockSpec((1, pc), lambda i: (0, i)),
                  pl.BlockSpec((h, REL_BUCKETS), lambda i: (0, 0))],
        out_specs=pl.BlockSpec((h, pc), lambda i: (0, i)),
        out_shape=jax.ShapeDtypeStruct((h, p), F32),
        compiler_params=_cparams(("arbitrary",)),
        name="bias_expand",
    )(bucket, tab_t)


def _t5_bucket_np(dist):
    n = np.maximum(dist, 0)
    exact = REL_BUCKETS // 2
    log_ratio = np.log(np.maximum(n, 1).astype(np.float32) / exact) / math.log(REL_MAX_DIST / exact)
    large = np.minimum(exact + (log_ratio * (REL_BUCKETS - exact)).astype(np.int32), REL_BUCKETS - 1)
    return np.where(n < exact, n, large).astype(np.int32)


def _toeplitz_buckets(rows, cols, deltas):
    i = np.arange(rows)[:, None]
    j = np.arange(cols)[None, :]
    tiles = [_t5_bucket_np(d + i - j) for d in deltas]
    tiles.append(np.full((rows, cols), REL_BUCKETS - 1, np.int32))
    return np.stack(tiles).reshape(1, -1)


def _dsa_kernel(cq_ref, ckv_ref, kw_ref, cqg_ref, ckvg_ref, wuq_ref, wuk_ref, wqi_ref, wuv_ref, bias_ref,
                o_ref, kvn_ref, kk_ref, key_ref, key_t_ref, ql_ref, lg_ref, mx_ref, ls_ref, p_ref, acc_ref, *, seq, n_top):
    qb = pl.program_id(1)
    q0 = qb * DSA_QB
    n_chunk_rows = seq // DSA_KC

    @pl.when(qb == 0)
    def _():
        lane = lax.broadcasted_iota(I32, (DSA_KC, LANES), 1)

        def prep(c, carry):
            rows = pl.ds(pl.multiple_of(c * DSA_KC, DSA_KC), DSA_KC)
            ckv = ckv_ref[rows, :]
            ms = jnp.mean(ckv * ckv, axis=-1, keepdims=True)
            kvn_ref[rows, :] = (ckv * lax.rsqrt(ms + LN_EPS) * ckvg_ref[...]).astype(BF16)
            kw = kw_ref[rows, :]
            kk_ref[rows, :] = jnp.where(lane < IDX_DIM, kw, pltpu.roll(kw, IDX_DIM, 1)).astype(BF16)
            return carry

        lax.fori_loop(0, n_chunk_rows, prep, 0)

    nc = (q0 + DSA_QB + DSA_KC - 1) // DSA_KC

    cq = cq_ref[...]
    ms = jnp.mean(cq * cq, axis=-1, keepdims=True)
    cqn = (cq * lax.rsqrt(ms + LN_EPS) * cqg_ref[...]).astype(BF16)

    lane_q = lax.broadcasted_iota(I32, (DSA_QB, LANES), 1)
    lo_half = lane_q < IDX_DIM
    row_i = lax.broadcasted_iota(I32, (DSA_QB, DSA_KC), 0)
    col_j = lax.broadcasted_iota(I32, (DSA_QB, DSA_KC), 1)
    qpos = q0 + row_i

    qi = _dot(cqn, wqi_ref[...]) * (IDX_DIM ** -0.5)
    wq = kw_ref[pl.ds(pl.multiple_of(q0, DSA_QB), DSA_QB), :] * (IDX_HEADS ** -0.5)
    q_heads = []
    for h in range(IDX_HEADS):
        qp = qi[:, (h // 2) * LANES:(h // 2 + 1) * LANES]
        keep = lo_half if h % 2 == 0 else jnp.logical_not(lo_half)
        q_heads.append(jnp.where(keep, qp, 0.0).astype(BF16))

    def score_body(c, carry):
        kc = kk_ref[pl.ds(pl.multiple_of(c * DSA_KC, DSA_KC), DSA_KC), :]
        s = jnp.zeros((DSA_QB, DSA_KC), F32)
        for h in range(IDX_HEADS):
            r = _dot_nt(q_heads[h], kc)
            s = s + wq[:, IDX_DIM + h:IDX_DIM + h + 1] * jnp.maximum(r, 0.0)
        s = jnp.where(qpos >= c * DSA_KC + col_j, s, -jnp.inf)
        key_ref[c] = _order_key(s)
        key_t_ref[c] = _order_key(s.T).reshape(DSA_KC // SUBLANES, SUBLANES, DSA_QB)
        return carry

    lax.fori_loop(0, nc, score_body, 0)

    n_grp = DSA_KC // SUBLANES
    kpos_t = (lax.broadcasted_iota(I32, (n_grp, SUBLANES, DSA_QB), 0) * SUBLANES
              + lax.broadcasted_iota(I32, (n_grp, SUBLANES, DSA_QB), 1))

    def count(pred):
        def chunk(c):
            one = jnp.where(pred(key_t_ref[c], c), 1.0, 0.0)
            parts = [one[g] for g in range(n_grp)]
            while len(parts) > 1:
                parts = [parts[i] + parts[i + 1] for i in range(0, len(parts), 2)]
            return parts[0]

        tot = lax.fori_loop(0, nc, lambda c, tot: tot + chunk(c), jnp.zeros((SUBLANES, DSA_QB), F32))
        for d in (4, 2, 1):
            tot = tot + pltpu.roll(tot, d, 0)
        return tot

    int_min = jnp.int32(-2 ** 31)
    k_top = float(n_top)
    cnt = count(lambda k, c: k >= 0)
    thr0 = jnp.where(cnt >= k_top, jnp.int32(0), int_min)

    def bit_body(i, thr):
        cand = thr | jnp.left_shift(jnp.int32(1), 30 - i)
        cnt = count(lambda k, c: k >= cand)
        return jnp.where(cnt >= k_top, cand, thr)

    thr_t = lax.fori_loop(0, 31, bit_body, thr0)

    need = k_top - count(lambda k, c: k > thr_t)
    n_eq = count(lambda k, c: k == thr_t)
    any_tie = jnp.max(jnp.where(n_eq > need, 1.0, 0.0)) > 0.0
    idx_bits = int(math.log2(seq))

    def tie_cut():
        def body(i, v):
            cand = v | jnp.left_shift(jnp.int32(1), idx_bits - 1 - i)
            cnt = count(lambda k, c: (k == thr_t) & (c * DSA_KC + kpos_t < cand))
            return jnp.where(cnt < need, cand, v)

        return lax.fori_loop(0, idx_bits, body, jnp.zeros((SUBLANES, DSA_QB), I32))

    cut_t = lax.cond(any_tie, tie_cut, lambda: jnp.full((SUBLANES, DSA_QB), seq, I32))

    def to_rows(v_t):
        def col(x):
            return jnp.broadcast_to(x.astype(F32), (DSA_QB, DSA_QB)).T.astype(I32)

        v = v_t[0:1, :]
        r = (col(v >> 16) << 16) | col(v & 0xFFFF)
        return jnp.concatenate([r] * (DSA_KC // DSA_QB), axis=1)

    thr = to_rows(thr_t)
    cut = to_rows(cut_t)

    q = _dot(cqn, wuq_ref[...])
    for h in range(A_HEADS):
        qp = q[:, (h // 2) * LANES:(h // 2 + 1) * LANES]
        keep = lo_half if h % 2 == 0 else jnp.logical_not(lo_half)
        qm = jnp.where(keep, qp, 0.0).astype(BF16)
        ql_ref[h * DSA_QB:(h + 1) * DSA_QB, :] = (_dot(qm, wuk_ref[h // 2]) * (A_HEAD_DIM ** -0.5)).astype(BF16)
    mx_ref[...] = jnp.full(mx_ref.shape, NEG_INF, F32)
    ls_ref[...] = jnp.zeros(ls_ref.shape, F32)
    acc_ref[...] = jnp.zeros(acc_ref.shape, F32)

    def logit_body(c, carry):
        kv = kvn_ref[pl.ds(pl.multiple_of(c * DSA_KC, DSA_KC), DSA_KC), :]
        k = key_ref[c]
        kpos = c * DSA_KC + col_j
        sel = ((k > thr) | ((k == thr) & (kpos <= cut))) & (qpos >= kpos)
        add_mask = jnp.where(sel, 0.0, NEG_INF)
        t = jnp.minimum((q0 - c * DSA_KC) // DSA_QB, 3)
        s_all = _dot_nt(ql_ref[...], kv)
        for h in range(A_HEADS):
            rows = slice(h * DSA_QB, (h + 1) * DSA_QB)
            sm = s_all[rows, :] + bias_ref[t, h] + add_mask
            lg_ref[c, rows, :] = sm
            mx_ref[h] = jnp.maximum(mx_ref[h], jnp.maximum(sm[:, :LANES], sm[:, LANES:]))
        return carry

    _loop2(nc, lambda c: logit_body(c, 0))

    for h in range(A_HEADS):
        mx_ref[h] = jnp.broadcast_to(jnp.max(mx_ref[h], axis=1, keepdims=True), (DSA_QB, LANES))

    def pv_body(c, carry):
        kv = kvn_ref[pl.ds(pl.multiple_of(c * DSA_KC, DSA_KC), DSA_KC), :]
        for h in range(A_HEADS):
            rows = slice(h * DSA_QB, (h + 1) * DSA_QB)
            m = mx_ref[h]
            p_lo = jnp.exp(lg_ref[c, rows, :LANES] - m)
            p_hi = jnp.exp(lg_ref[c, rows, LANES:] - m)
            ls_ref[h] += p_lo + p_hi
            p_ref[rows, :LANES] = p_lo.astype(BF16)
            p_ref[rows, LANES:] = p_hi.astype(BF16)
        acc_ref[...] += _dot(p_ref[...], kv)
        return carry

    _loop2(nc, lambda c: pv_body(c, 0))

    for pr in range(A_HEADS // 2):
        o = jnp.zeros((DSA_QB, LANES), F32)
        for h in (2 * pr, 2 * pr + 1):
            l = jnp.sum(ls_ref[h], axis=1, keepdims=True)
            o_lat = (acc_ref[h * DSA_QB:(h + 1) * DSA_QB, :] / l).astype(BF16)
            o = o + _dot(o_lat, wuv_ref[h])
        o_ref[:, pr * LANES:(pr + 1) * LANES] = o.astype(o_ref.dtype)


def _dsa(hm, cq_gain, ckv_gain, w_uq, w_uk, w_uv, w_qidx, bias_tiles):
    bsz, seq, _ = hm.shape
    n_top = min(DSA_TOPK, seq // 4)
    hd = A_HEADS * A_HEAD_DIM
    wuq = w_uq.reshape(A_Q_RANK, hd).astype(BF16)
    wuk = jnp.transpose(w_uk, (1, 2, 0)).reshape(A_HEADS // 2, 2 * A_HEAD_DIM, A_KV_RANK).astype(BF16)
    wqi = w_qidx.reshape(A_Q_RANK, IDX_HEADS * IDX_DIM).astype(BF16)
    wv = jnp.transpose(w_uv, (1, 0, 2))
    zeros = jnp.zeros_like(wv)
    even = (jnp.arange(A_HEADS) % 2 == 0)[:, None, None]
    wuv = jnp.concatenate([jnp.where(even, wv, zeros), jnp.where(even, zeros, wv)], axis=-1).astype(BF16)
    kernel = functools.partial(_dsa_kernel, seq=seq, n_top=n_top)
    const2 = lambda b, q: (0, 0)
    const3 = lambda b, q: (0, 0, 0)
    return pl.pallas_call(
        kernel,
        grid=(bsz, seq // DSA_QB),
        in_specs=[
            pl.BlockSpec((None, DSA_QB, A_Q_RANK), lambda b, q: (b, q, CQ_OFF // A_Q_RANK)),
            pl.BlockSpec((None, seq, A_KV_RANK), lambda b, q: (b, 0, CKV_OFF // A_KV_RANK)),
            pl.BlockSpec((None, seq, LANES), lambda b, q: (b, 0, KW_OFF // LANES)),
            pl.BlockSpec((1, A_Q_RANK), const2),
            pl.BlockSpec((1, A_KV_RANK), const2),
            pl.BlockSpec(wuq.shape, const2),
            pl.BlockSpec(wuk.shape, const3),
            pl.BlockSpec(wqi.shape, const2),
            pl.BlockSpec(wuv.shape, const3),
            pl.BlockSpec(bias_tiles.shape, lambda b, q: (0, 0, 0, 0)),
        ],
        out_specs=pl.BlockSpec((None, DSA_QB, hd), lambda b, q: (b, q, 0)),
        out_shape=jax.ShapeDtypeStruct((bsz, seq, hd), BF16),
        scratch_shapes=[
            pltpu.VMEM((seq, A_KV_RANK), BF16),
            pltpu.VMEM((seq, LANES), BF16),
            pltpu.VMEM((seq // DSA_KC, DSA_QB, DSA_KC), I32),
            pltpu.VMEM((seq // DSA_KC, DSA_KC // SUBLANES, SUBLANES, DSA_QB), I32),
            pltpu.VMEM((A_HEADS * DSA_QB, A_KV_RANK), BF16),
            pltpu.VMEM((seq // DSA_KC, A_HEADS * DSA_QB, DSA_KC), F32),
            pltpu.VMEM((A_HEADS, DSA_QB, LANES), F32),
            pltpu.VMEM((A_HEADS, DSA_QB, LANES), F32),
            pltpu.VMEM((A_HEADS * DSA_QB, DSA_KC), BF16),
            pltpu.VMEM((A_HEADS * DSA_QB, A_KV_RANK), F32),
        ],
        compiler_params=_cparams(("arbitrary", "arbitrary")),
        name="dsa",
    )(hm, hm, hm, cq_gain.reshape(1, -1), ckv_gain.reshape(1, -1), wuq, wuk, wqi, wuv, bias_tiles)


S5_POWERS = (1,)
S5_ROWS = 2 * len(S5_POWERS) + 2


def _s5_prep_kernel(lre_ref, lim_ref, ls_ref, bre_ref, bim_ref, tab_ref, bbre_ref, bbim_ref):
    lre = lre_ref[...]
    lim = lim_ref[...]
    step = jnp.exp(ls_ref[...])
    for r, k in enumerate(S5_POWERS):
        mag = jnp.exp(k * lre * step)
        ang = k * lim * step
        tab_ref[2 * r:2 * r + 1, :] = mag * jnp.cos(ang)
        tab_ref[2 * r + 1:2 * r + 2, :] = mag * jnp.sin(ang)
    a = tab_ref[0:1, :] - 1.0
    b = tab_ref[1:2, :]
    den = lre * lre + lim * lim
    f_re = (a * lre + b * lim) / den
    f_im = (b * lre - a * lim) / den
    base = 2 * len(S5_POWERS)
    tab_ref[base:base + 1, :] = f_re
    tab_ref[base + 1:base + 2, :] = f_im
    bre = bre_ref[...]
    bim = bim_ref[...]
    bbre_ref[...] = f_re * bre - f_im * bim
    bbim_ref[...] = f_re * bim + f_im * bre


def _s5_prep(lam_re, lam_im, log_step, b_re, b_im):
    lre = lam_re.reshape(1, NS)
    lim = lam_im.reshape(1, NS)
    ls = jnp.repeat(log_step, SSM_STATE).reshape(1, NS)
    bre = jnp.transpose(b_re, (2, 0, 1)).reshape(SSM_GROUP, NS)
    bim = jnp.transpose(b_im, (2, 0, 1)).reshape(SSM_GROUP, NS)
    return pl.pallas_call(
        _s5_prep_kernel,
        out_shape=(jax.ShapeDtypeStruct((S5_ROWS, NS), F32),
                   jax.ShapeDtypeStruct((SSM_GROUP, NS), F32),
                   jax.ShapeDtypeStruct((SSM_GROUP, NS), F32)),
        name="s5_prep",
    )(lre, lim, ls, bre, bim)


def _s5_kernel(u_ref, perm_ref, perm_t_ref, bm_ref, cm_ref, lam_ref, dsk_ref, wg_ref, bg_ref, o_ref, xs_ref, c_ref,
               *, lane_chunk, unroll):
    tc = pl.program_id(1)
    rows_all = SUBLANES * S5_LT

    @pl.when(tc == 0)
    def _():
        c_ref[...] = jnp.zeros(c_ref.shape, F32)

    u = u_ref[...].reshape(rows_all, SSM_WIDTH)
    u_tm = _dot(perm_ref[...], u.astype(BF16)).astype(BF16)
    n_sg = SSM_WIDTH // LANES
    sg_w = NS // n_sg
    for sg in range(n_sg):
        bu = _dot(u_tm[:, sg * LANES:(sg + 1) * LANES], bm_ref[sg])
        xs_ref[:, sg * sg_w:(sg + 1) * sg_w] = bu[:, :sg_w]
        xs_ref[:, NS + sg * sg_w:NS + (sg + 1) * sg_w] = bu[:, sg_w:]

    for lc in range(NS // lane_chunk):
        re_cols = slice(lc * lane_chunk, (lc + 1) * lane_chunk)
        im_cols = slice(NS + lc * lane_chunk, NS + (lc + 1) * lane_chunk)
        l_re = lam_ref[0, :, re_cols]
        l_im = lam_ref[1, :, re_cols]

        def step_body(i, carry, re_cols=re_cols, im_cols=im_cols, l_re=l_re, l_im=l_im):
            x_re, x_im = carry
            for k in range(unroll):
                rows = pl.ds(pl.multiple_of((i * unroll + k) * SUBLANES, SUBLANES), SUBLANES)
                x_re, x_im = (l_re * x_re - l_im * x_im + xs_ref[rows, re_cols],
                              l_re * x_im + l_im * x_re + xs_ref[rows, im_cols])
                xs_ref[rows, re_cols] = x_re
                xs_ref[rows, im_cols] = x_im
            return x_re, x_im

        x_re, x_im = lax.fori_loop(0, S5_LT // unroll, step_body, (c_ref[:, re_cols], c_ref[:, im_cols]))
        c_ref[:, re_cols] = x_re
        c_ref[:, im_cols] = x_im

    y_tm = jnp.concatenate(
        [_dot(xs_ref[:, sg * sg_w:(sg + 1) * sg_w].astype(BF16), cm_ref[sg, :sg_w, :])
         + _dot(xs_ref[:, NS + sg * sg_w:NS + (sg + 1) * sg_w].astype(BF16), cm_ref[sg, sg_w:, :])
         for sg in range(n_sg)], axis=1)
    y1 = y_tm.astype(BF16)
    r1 = y_tm - y1.astype(F32)
    y2 = r1.astype(BF16)
    y3 = (r1 - y2.astype(F32)).astype(BF16)
    pt = perm_t_ref[...]
    y = _dot(pt, y1) + (_dot(pt, y2) + _dot(pt, y3)) + dsk_ref[...] * u
    y = jax.nn.gelu(y)
    z = _dot(y.astype(BF16), wg_ref[...]) + bg_ref[...]
    o_ref[...] = (y * jax.nn.sigmoid(z)).astype(o_ref.dtype).reshape(o_ref.shape)


def _s5(hm, tab, bb_re, bb_im, c_re, c_im, d_skip, w_glu, b_glu, lane_chunk=1024):
    bsz, seq, _ = hm.shape
    eye = jnp.eye(SSM_GROUPS, dtype=F32)

    def in_map(bb):
        bb = jnp.transpose(bb.reshape(SSM_GROUP, SSM_GROUPS, SSM_STATE), (1, 0, 2))
        return jnp.einsum('gpn,gh->gphn', bb, eye).reshape(SSM_WIDTH, NS)

    def out_map(c):
        return jnp.einsum('gpn,gh->gnhp', c, eye).reshape(NS, SSM_WIDTH)

    n_sg = SSM_WIDTH // LANES
    sg_w = NS // n_sg

    def diag(m, rows, cols):
        return jnp.stack([m[s * rows:(s + 1) * rows, s * cols:(s + 1) * cols] for s in range(n_sg)])

    bm = jnp.concatenate([diag(in_map(bb_re), LANES, sg_w), diag(in_map(bb_im), LANES, sg_w)], axis=2).astype(BF16)
    cm = jnp.concatenate([diag(out_map(c_re), sg_w, LANES), -diag(out_map(c_im), sg_w, LANES)], axis=1).astype(BF16)
    lam = jnp.broadcast_to(tab[0:2, None, :], (2, SUBLANES, NS))
    rows_all = SUBLANES * S5_LT
    r = np.arange(rows_all)
    perm = np.zeros((rows_all, rows_all), np.float32)
    perm[r, (r % SUBLANES) * S5_LT + r // SUBLANES] = 1.0
    perm_t = jnp.asarray(perm.T, BF16)
    perm = jnp.asarray(perm, BF16)
    assert bsz % SUBLANES == 0 and seq % S5_LT == 0
    kernel = functools.partial(_s5_kernel, lane_chunk=lane_chunk, unroll=4)
    const2 = lambda b, t: (0, 0)
    once = dict(pipeline_mode=pl.Buffered(1))
    return pl.pallas_call(
        kernel,
        grid=(bsz // SUBLANES, seq // S5_LT),
        in_specs=[
            pl.BlockSpec((SUBLANES, S5_LT, SSM_WIDTH), lambda b, t: (b, t, U_OFF // SSM_WIDTH)),
            pl.BlockSpec(perm.shape, const2, **once),
            pl.BlockSpec(perm.shape, const2, **once),
            pl.BlockSpec(bm.shape, lambda b, t: (0, 0, 0), **once),
            pl.BlockSpec(cm.shape, lambda b, t: (0, 0, 0), **once),
            pl.BlockSpec(lam.shape, lambda b, t: (0, 0, 0), **once),
            pl.BlockSpec((1, SSM_WIDTH), const2),
            pl.BlockSpec((SSM_WIDTH, SSM_WIDTH), const2),
            pl.BlockSpec((1, SSM_WIDTH), const2),
        ],
        out_specs=pl.BlockSpec((SUBLANES, S5_LT, SSM_WIDTH), lambda b, t: (b, t, 0)),
        out_shape=jax.ShapeDtypeStruct((bsz, seq, SSM_WIDTH), BF16),
        scratch_shapes=[pltpu.VMEM((rows_all, 2 * NS), F32),
                        pltpu.VMEM((SUBLANES, 2 * NS), F32)],
        compiler_params=_cparams(("arbitrary", "arbitrary")),
        name="s5",
    )(hm, perm, perm_t, bm, cm, lam, d_skip.reshape(1, -1), w_glu.astype(BF16), b_glu.reshape(1, -1))


def _moba_kernel(q_ref, k_ref, v_ref, bias_ref, o_ref, kb_ref, vb_ref, km_ref, lg_ref, mx_ref, ls_ref, p_ref, acc_ref,
                 *, n_blk):
    qb = pl.program_id(2)
    blk = MOBA_BLOCK
    heads = range(2 * MOBA_PAIRS)

    def pair_lanes(pr):
        return slice(pr * LANES, (pr + 1) * LANES)

    @pl.when(qb == 0)
    def _():
        km_ref[...] = jnp.zeros(km_ref.shape, F32)
        lane_k = lax.broadcasted_iota(I32, (blk, LANES), 1)
        for pr in range(MOBA_PAIRS):
            for n in range(n_blk):
                rows = slice(n * blk, (n + 1) * blk)
                kblk = k_ref[rows, pair_lanes(pr)]
                km_ref[pr, n:n + 1, :] = jnp.mean(kblk, axis=0, keepdims=True)
                kb_ref[2 * pr, rows, :] = jnp.where(lane_k < C_HEAD_DIM, kblk,
                                                    jnp.where(lane_k == C_HEAD_DIM + n, 1.0, 0.0)).astype(BF16)
                kb_ref[2 * pr + 1, rows, :] = jnp.where(lane_k >= C_HEAD_DIM, kblk,
                                                        jnp.where(lane_k == n, 1.0, 0.0)).astype(BF16)
                vb_ref[pr, rows, :] = v_ref[rows, pair_lanes(pr)].astype(BF16)

    lane = lax.broadcasted_iota(I32, (blk, LANES), 1)
    row_i = lax.broadcasted_iota(I32, (blk, blk), 0)
    col_j = lax.broadcasted_iota(I32, (blk, blk), 1)
    causal_add = jnp.where(row_i >= col_j, 0.0, NEG_INF)
    blk_id = lax.broadcasted_iota(I32, (SUBLANES, blk), 0)
    blk_f = blk_id.astype(F32)
    lane_m = lax.broadcasted_iota(I32, (SUBLANES, LANES), 1)
    scale = C_HEAD_DIM ** -0.5
    own_rows = pl.ds(pl.multiple_of(qb * blk, blk), blk)

    def over_blocks(x, op):
        for d in (4, 2, 1):
            x = op(x, pltpu.roll(x, d, 0))
        return x

    qa = []
    for pr in range(MOBA_PAIRS):
        q = q_ref[:, pair_lanes(pr)]
        km = km_ref[pr]
        km_both = jnp.concatenate([jnp.where(lane_m < C_HEAD_DIM, km, 0.0), jnp.where(lane_m >= C_HEAD_DIM, km, 0.0)],
                                  axis=0)
        km_hi = km_both.astype(BF16)
        km_lo = (km_both - km_hi.astype(F32)).astype(BF16)
        q_hi = q.astype(BF16)
        q_lo = (q - q_hi.astype(F32)).astype(BF16)
        gate_both = _dot_nt(km_hi, q_hi) + (_dot_nt(km_hi, q_lo) + _dot_nt(km_lo, q_hi))
        for s in range(2):
            hd = 2 * pr + s
            keep = (lane < C_HEAD_DIM) if s == 0 else (lane >= C_HEAD_DIM)
            gate = gate_both[s * SUBLANES:(s + 1) * SUBLANES, :]
            g = jnp.where(blk_id < qb, gate, -jnp.inf)
            sel = jnp.zeros((SUBLANES, blk), F32)
            for _ in range(MOBA_TOPK):
                mx = over_blocks(g, jnp.maximum)
                is_max = (g == mx) & (g > -jnp.inf)
                first = over_blocks(jnp.where(is_max, blk_f, float(SUBLANES)), jnp.minimum)
                pick = blk_f == first
                sel = jnp.where(pick, 1.0, sel)
                g = jnp.where(pick, -jnp.inf, g)
            add_t = jnp.where((sel > 0.5) | (blk_id == qb), 0.0, NEG_INF)
            lead = C_HEAD_DIM if s == 0 else 0
            pieces = ([jnp.zeros((lead, blk), F32)] if lead else []) + [add_t]
            pieces.append(jnp.zeros((LANES - lead - SUBLANES, blk), F32))
            add_q = jnp.concatenate(pieces, axis=0).T
            qa.append(jnp.where(keep, q * scale, add_q).astype(BF16))
            sm = _dot_nt(qa[hd], kb_ref[hd, own_rows, :]) + bias_ref[0, hd] + causal_add
            lg_ref[hd, qb] = sm
            mx_ref[hd] = jnp.maximum(sm[:, :LANES], sm[:, LANES:])
            ls_ref[hd] = jnp.zeros((blk, LANES), F32)
            acc_ref[hd] = jnp.zeros((blk, LANES), F32)

    def logit_body(n):
        rows = pl.ds(pl.multiple_of(n * blk, blk), blk)
        t = jnp.minimum(qb - n, 2)
        for hd in heads:
            sm = _dot_nt(qa[hd], kb_ref[hd, rows, :]) + bias_ref[t, hd]
            lg_ref[hd, n] = sm
            mx_ref[hd] = jnp.maximum(mx_ref[hd], jnp.maximum(sm[:, :LANES], sm[:, LANES:]))

    _loop2(qb, logit_body)

    for hd in heads:
        mx_ref[hd] = jnp.broadcast_to(jnp.max(mx_ref[hd], axis=1, keepdims=True), (blk, LANES))

    def pv_body(n):
        rows = pl.ds(pl.multiple_of(n * blk, blk), blk)
        for hd in heads:
            m = mx_ref[hd]
            p_lo = jnp.exp(lg_ref[hd, n, :, :LANES] - m)
            p_hi = jnp.exp(lg_ref[hd, n, :, LANES:] - m)
            ls_ref[hd] += p_lo + p_hi
            p_ref[hd, :, :LANES] = p_lo.astype(BF16)
            p_ref[hd, :, LANES:] = p_hi.astype(BF16)
            acc_ref[hd] += _dot(p_ref[hd], vb_ref[hd // 2, rows, :])

    _loop2(qb + 1, pv_body)

    for pr in range(MOBA_PAIRS):
        o = jnp.where(lane < C_HEAD_DIM,
                      acc_ref[2 * pr] / jnp.sum(ls_ref[2 * pr], axis=1, keepdims=True),
                      acc_ref[2 * pr + 1] / jnp.sum(ls_ref[2 * pr + 1], axis=1, keepdims=True))
        o_ref[:, pair_lanes(pr)] = o.astype(o_ref.dtype)


def _moba(hm, bias_tiles):
    bsz, seq, _ = hm.shape
    n_blk = seq // MOBA_BLOCK
    hs = 2 * MOBA_PAIRS
    wide = MOBA_PAIRS * LANES
    n_grp = C_HEADS // hs
    assert n_blk <= SUBLANES
    kernel = functools.partial(_moba_kernel, n_blk=n_blk)
    return pl.pallas_call(
        kernel,
        grid=(bsz, n_grp, n_blk),
        in_specs=[
            pl.BlockSpec((None, MOBA_BLOCK, wide), lambda b, h, q: (b, q, Q_OFF // wide + h)),
            pl.BlockSpec((None, seq, wide), lambda b, h, q: (b, 0, K_OFF // wide + h)),
            pl.BlockSpec((None, seq, wide), lambda b, h, q: (b, 0, V_OFF // wide + h)),
            pl.BlockSpec((3, hs, MOBA_BLOCK, MOBA_BLOCK), lambda b, h, q: (0, h, 0, 0)),
        ],
        out_specs=pl.BlockSpec((None, MOBA_BLOCK, wide), lambda b, h, q: (b, q, h)),
        out_shape=jax.ShapeDtypeStruct((bsz, seq, C_HEADS * C_HEAD_DIM), BF16),
        scratch_shapes=[
            pltpu.VMEM((hs, seq, LANES), BF16),
            pltpu.VMEM((MOBA_PAIRS, seq, LANES), BF16),
            pltpu.VMEM((MOBA_PAIRS, SUBLANES, LANES), F32),
            pltpu.VMEM((hs, n_blk, MOBA_BLOCK, MOBA_BLOCK), F32),
            pltpu.VMEM((hs, MOBA_BLOCK, LANES), F32),
            pltpu.VMEM((hs, MOBA_BLOCK, LANES), F32),
            pltpu.VMEM((hs, MOBA_BLOCK, MOBA_BLOCK), BF16),
            pltpu.VMEM((hs, MOBA_BLOCK, LANES), F32),
        ],
        compiler_params=_cparams(("arbitrary", "arbitrary", "arbitrary")),
        name="moba",
    )(hm, hm, hm, bias_tiles)


def _merge_kernel(xb_ref, x_ref, oa_ref, ob_ref, oc_ref, wg0_ref, wg1_ref, wg2_ref, wb_ref, wo_ref, g_ref, b_ref,
                  o_ref, o16_ref, mg_ref):
    j = pl.program_id(1)
    tn = wg0_ref.shape[1]
    xb = xb_ref[...]
    merged = None
    for n, (o_n, wg_n) in enumerate(((oa_ref, wg0_ref), (ob_ref, wg1_ref), (oc_ref, wg2_ref))):
        gate = jax.nn.sigmoid(_dot(xb, wg_n[...]))
        y = _dot(o_n[...], wb_ref[n])
        merged = gate * y if merged is None else merged + gate * y
    mg_ref[:, pl.ds(pl.multiple_of(j * tn, tn), tn)] = merged.astype(BF16)

    @pl.when(j == pl.num_programs(1) - 1)
    def _():
        half = o_ref.shape[0] // 2
        for rows in (slice(0, half), slice(half, 2 * half)):
            z = DEEPNORM_ALPHA * x_ref[rows, :] + _dot(mg_ref[rows, :], wo_ref[...])
            y = _layer_norm(z, g_ref[...], b_ref[...])
            o_ref[rows, :] = y
            o16_ref[rows, :] = y.astype(BF16)


def _merge(xb, x2d, oa, ob, oc, wg, wb, wo, ln_g, ln_b, tm=512, tn=256):
    n, d = x2d.shape
    tm = min(tm, n)
    nd = d // tn
    bw = oa.shape[1]
    row = lambda i, j: (i, 0)
    const2 = lambda i, j: (0, 0)
    return pl.pallas_call(
        _merge_kernel,
        grid=(n // tm, nd),
        in_specs=[
            pl.BlockSpec((tm, d), row),
            pl.BlockSpec((tm, d), row),
            pl.BlockSpec((tm, bw), row),
            pl.BlockSpec((tm, bw), row),
            pl.BlockSpec((tm, bw), row),
            pl.BlockSpec((d, tn), lambda i, j: (0, j)),
            pl.BlockSpec((d, tn), lambda i, j: (0, nd + j)),
            pl.BlockSpec((d, tn), lambda i, j: (0, 2 * nd + j)),
            pl.BlockSpec((N_BRANCH, bw, tn), lambda i, j: (0, 0, j)),
            pl.BlockSpec((d, d), const2, pipeline_mode=pl.Buffered(1)),
            pl.BlockSpec((1, d), const2),
            pl.BlockSpec((1, d), const2),
        ],
        out_specs=[pl.BlockSpec((tm, d), row), pl.BlockSpec((tm, d), row)],
        out_shape=[jax.ShapeDtypeStruct((n, d), F32), jax.ShapeDtypeStruct((n, d), BF16)],
        scratch_shapes=[pltpu.VMEM((tm, d), BF16)],
        compiler_params=_cparams(("arbitrary", "arbitrary")),
        name="merge",
    )(xb, x2d, oa, ob, oc, wg, wg, wg, wb, wo, ln_g.reshape(1, -1), ln_b.reshape(1, -1))


def _ffn_up_kernel(x_ref, wa_ref, wv_ref, cwa_ref, cwv_ref, cba_ref, cbv_ref, o_ref, tail_ref,
                   *, tiles_per_seq, sub_tiles):
    i = pl.program_id(1)
    tm = x_ref.shape[0]

    @pl.when(i % tiles_per_seq == 0)
    def _():
        tail_ref[...] = jnp.zeros(tail_ref.shape, F32)

    xb = x_ref[...]
    last = SUBLANES - 1

    def conv(part, cols, w_ref, cw_ref, cb_ref):
        h = _dot(xb, w_ref[:, cols])
        hh = tail_ref[part, :, cols]
        tail_ref[part, :, cols] = h[tm - SUBLANES:tm, :]
        w0, w1, w2, cb = cw_ref[0:1, cols], cw_ref[1:2, cols], cw_ref[2:3, cols], cb_ref[:, cols]
        body = cb + w0 * pltpu.roll(h, 2, 0) + w1 * pltpu.roll(h, 1, 0) + w2 * h
        h8 = h[0:SUBLANES, :]
        row = lax.broadcasted_iota(I32, h8.shape, 0)
        p1 = jnp.where(row == 0, hh[last:last + 1, :], pltpu.roll(h8, 1, 0))
        p2 = jnp.where(row == 0, hh[last - 1:last, :], jnp.where(row == 1, hh[last:last + 1, :], pltpu.roll(h8, 2, 0)))
        head = cb + w0 * p2 + w1 * p1 + w2 * h8
        return jnp.concatenate([head, body[SUBLANES:, :]], axis=0)

    for lo, hi in sub_tiles:
        cols = slice(lo, hi)
        a = conv(0, cols, wa_ref, cwa_ref, cba_ref)
        val = conv(1, cols, wv_ref, cwv_ref, cbv_ref)
        o_ref[:, cols] = (jax.nn.gelu(a) * val).astype(o_ref.dtype)


def _ffn_up(xb, seq, w_up, conv_w, conv_b, tm=512, tn=1408):
    n, d = xb.shape
    d_ff = w_up.shape[1] // 2
    tm = min(tm, seq)
    nf = d_ff // tn
    assert nf * tn == d_ff and tn % LANES == 0
    sub_tiles = ((0, tn),)
    kernel = functools.partial(_ffn_up_kernel, tiles_per_seq=seq // tm, sub_tiles=sub_tiles)
    return pl.pallas_call(
        kernel,
        grid=(nf, n // tm),
        in_specs=[
            pl.BlockSpec((tm, d), lambda j, i: (i, 0)),
            pl.BlockSpec((d, tn), lambda j, i: (0, j)),
            pl.BlockSpec((d, tn), lambda j, i: (0, nf + j)),
            pl.BlockSpec((CONV_WIDTH, tn), lambda j, i: (0, j)),
            pl.BlockSpec((CONV_WIDTH, tn), lambda j, i: (0, nf + j)),
            pl.BlockSpec((1, tn), lambda j, i: (0, j)),
            pl.BlockSpec((1, tn), lambda j, i: (0, nf + j)),
        ],
        out_specs=pl.BlockSpec((tm, tn), lambda j, i: (i, j)),
        out_shape=jax.ShapeDtypeStruct((n, d_ff), BF16),
        scratch_shapes=[pltpu.VMEM((2, SUBLANES, tn), F32)],
        compiler_params=_cparams(("arbitrary", "arbitrary")),
        name="ffn_up",
    )(xb, w_up, w_up, conv_w, conv_w, conv_b.reshape(1, -1), conv_b.reshape(1, -1))


def _ffn_down_kernel(a_ref, w_ref, x_ref, g_ref, b_ref, o_ref, o16_ref):
    half = o_ref.shape[0] // 2
    for rows in (slice(0, half), slice(half, 2 * half)):
        z = DEEPNORM_ALPHA * x_ref[rows, :] + _dot(a_ref[rows, :], w_ref[...])
        y = _layer_norm(z, g_ref[...], b_ref[...])
        o_ref[rows, :] = y
        o16_ref[rows, :] = y.astype(BF16)


def _ffn_down(act, w_down, x2d, ln_g, ln_b, tm=256):
    n, d = x2d.shape
    d_ff = act.shape[1]
    tm = min(tm, n)
    row = lambda i: (i, 0)
    const2 = lambda i: (0, 0)
    return pl.pallas_call(
        _ffn_down_kernel,
        grid=(n // tm,),
        in_specs=[
            pl.BlockSpec((tm, d_ff), row),
            pl.BlockSpec((d_ff, d), const2, pipeline_mode=pl.Buffered(1)),
            pl.BlockSpec((tm, d), row),
            pl.BlockSpec((1, d), const2),
            pl.BlockSpec((1, d), const2),
        ],
        out_specs=[pl.BlockSpec((tm, d), row), pl.BlockSpec((tm, d), row)],
        out_shape=[jax.ShapeDtypeStruct((n, d), F32), jax.ShapeDtypeStruct((n, d), BF16)],
        compiler_params=_cparams(("arbitrary",)),
        name="ffn_down",
    )(act, w_down, x2d, ln_g.reshape(1, -1), ln_b.reshape(1, -1))


def _mixer_weight(w_in_l):
    d = w_in_l.shape[0]
    sizes = (A_Q_RANK, A_KV_RANK, IDX_DIM, IDX_HEADS, SSM_WIDTH, 512, 512, 512)
    offs = np.cumsum((0,) + sizes)
    seg = [w_in_l[:, offs[i]:offs[i + 1]] for i in range(len(sizes))]
    pad = lambda w: jnp.zeros((d, w), w_in_l.dtype)
    cols = [seg[0], pad(CKV_OFF - A_Q_RANK), seg[1], seg[2], seg[3], pad(U_OFF - KW_OFF - IDX_DIM - IDX_HEADS),
            seg[4], seg[5], seg[6], seg[7]]
    w = jnp.concatenate(cols, axis=1)
    assert w.shape[1] == PROJ_W
    return w.astype(BF16), w_in_l[:, offs[-1]:].astype(BF16)


def kernel(x, rel_bias, w_in, cq_gain, ckv_gain, w_uq, w_uk, w_uv, w_qidx, lam_re, lam_im, log_step, b_re, b_im, c_re, c_im, d_skip, w_glu, b_glu, w_branch, w_out, ln1_g, ln1_b, w_up, conv_w, conv_b, w_down, ln2_g, ln2_b):
    bsz, seq, d = x.shape
    assert seq % MOBA_BLOCK == 0 and seq % DSA_KC == 0
    n = bsz * seq

    tab_t = jnp.transpose(rel_bias)
    dsa_bkt = jnp.asarray(_toeplitz_buckets(DSA_QB, DSA_KC, (0, DSA_QB, 2 * DSA_QB)))
    moba_bkt = jnp.asarray(_toeplitz_buckets(MOBA_BLOCK, MOBA_BLOCK, (0, MOBA_BLOCK)))
    dsa_bias = _bias_expand(dsa_bkt, tab_t[:A_HEADS]).reshape(A_HEADS, 4, DSA_QB, DSA_KC)
    dsa_bias = jnp.transpose(dsa_bias, (1, 0, 2, 3))
    moba_bias = _bias_expand(moba_bkt, tab_t[A_HEADS:]).reshape(C_HEADS, 3, MOBA_BLOCK, MOBA_BLOCK)
    moba_bias = jnp.transpose(moba_bias, (1, 0, 2, 3))

    x2d = x.reshape(n, d)
    xb = x2d.astype(BF16)
    for l in range(DEPTH):
        w_mix, w_gate = _mixer_weight(w_in[l])
        hm = _proj(xb, w_mix).reshape(bsz, seq, PROJ_W)
        o_a = _dsa(hm, cq_gain[l], ckv_gain[l], w_uq[l], w_uk[l], w_uv[l], w_qidx[l], dsa_bias)
        tab, bb_re, bb_im = _s5_prep(lam_re[l], lam_im[l], log_step[l], b_re[l], b_im[l])
        o_b = _s5(hm, tab, bb_re, bb_im, c_re[l], c_im[l], d_skip[l], w_glu[l], b_glu[l])
        o_c = _moba(hm, moba_bias)
        bw = o_a.shape[-1]
        x2d, xb = _merge(xb, x2d, o_a.reshape(n, bw), o_b.reshape(n, bw), o_c.reshape(n, bw), w_gate,
                         w_branch[l].astype(BF16), w_out[l].astype(BF16), ln1_g[l], ln1_b[l])
        act = _ffn_up(xb, seq, w_up[l].astype(BF16), conv_w[l], conv_b[l])
        x2d, xb = _ffn_down(act, w_down[l].astype(BF16), x2d, ln2_g[l], ln2_b[l])
    return x2d.reshape(bsz, seq, d)
```

```python
import functools
import math

import numpy as np
import jax
import jax.numpy as jnp
from jax import lax
from jax.experimental import pallas as pl
from jax.experimental.pallas import tpu as pltpu

F32 = jnp.float32
BF16 = jnp.bfloat16
I32 = jnp.int32

DEPTH = 2
A_HEADS = 8
A_HEAD_DIM = 64
A_Q_RANK = 384
A_KV_RANK = 256
IDX_HEADS = 16
IDX_DIM = 64
DSA_TOPK = 256
SSM_GROUP = 16
SSM_GROUPS = 32
SSM_WIDTH = SSM_GROUP * SSM_GROUPS
SSM_STATE = 64
C_HEADS = 8
C_HEAD_DIM = 64
MOBA_BLOCK = 256
MOBA_TOPK = 3
N_BRANCH = 3
CONV_WIDTH = 3
REL_BUCKETS = 32
REL_MAX_DIST = 128
LN_EPS = 1e-5
NEG_INF = -1e30
DEEPNORM_ALPHA = (2 * DEPTH) ** 0.25

LANES = 128
SUBLANES = 8
VMEM_LIMIT = 56 * 1024 * 1024

CQ_OFF, CKV_OFF, KW_OFF, U_OFF, Q_OFF, K_OFF, V_OFF, PROJ_W = 0, 512, 768, 1024, 1536, 2048, 2560, 3072

DSA_QB = 256
DSA_KC = 256
S5_LT = 64
MOBA_PAIRS = 2
NS = SSM_GROUPS * SSM_STATE


def _cparams(sem):
    return pltpu.CompilerParams(dimension_semantics=sem, vmem_limit_bytes=VMEM_LIMIT)


def _dot(a, b):
    return jnp.dot(a, b, preferred_element_type=F32)


def _dot_nt(a, b):
    return lax.dot_general(a, b, (((1,), (1,)), ((), ())), preferred_element_type=F32)


def _loop2(n, body):
    def pair(i, carry):
        body(2 * i)
        body(2 * i + 1)
        return carry

    lax.fori_loop(0, lax.shift_right_logical(n, 1), pair, 0)

    @pl.when((n & 1) == 1)
    def _():
        body(n - 1)


def _order_key(s):
    bits = lax.bitcast_convert_type(s, I32)
    return bits ^ ((bits >> 31) & 0x7FFFFFFF)


def _layer_norm(z, g, b):
    mu = jnp.mean(z, axis=-1, keepdims=True)
    zc = z - mu
    var = jnp.mean(zc * zc, axis=-1, keepdims=True)
    return zc * lax.rsqrt(var + LN_EPS) * g + b


def _proj_kernel(x_ref, w_ref, o_ref):
    o_ref[...] = _dot(x_ref[...].astype(BF16), w_ref[...])


def _proj(xb, w, tm=1024, tn=1024):
    n, d = xb.shape
    pw = w.shape[1]
    tm = min(tm, n)
    return pl.pallas_call(
        _proj_kernel,
        grid=(n // tm, pw // tn),
        in_specs=[pl.BlockSpec((tm, d), lambda i, j: (i, 0)),
                  pl.BlockSpec((d, tn), lambda i, j: (0, j))],
        out_specs=pl.BlockSpec((tm, tn), lambda i, j: (i, j)),
        out_shape=jax.ShapeDtypeStruct((n, pw), F32),
        compiler_params=_cparams(("arbitrary", "arbitrary")),
        name="proj",
    )(xb, w)


def _bias_kernel(bkt_ref, tab_ref, o_ref):
    b = bkt_ref[...]
    acc = jnp.zeros(o_ref.shape, F32)
    for k in range(REL_BUCKETS):
        acc = jnp.where(b == k, tab_ref[:, k:k + 1], acc)
    o_ref[...] = acc


def _bias_expand(bucket, tab_t, pc=8192):
    p = bucket.shape[1]
    h = tab_t.shape[0]
    return pl.pallas_call(
        _bias_kernel,
        grid=(p // pc,),
        in_specs=[pl.BlockSpec((1, pc), lambda i: (0, i)),
                  pl.BlockSpec((h, REL_BUCKETS), lambda i: (0, 0))],
        out_specs=pl.BlockSpec((h, pc), lambda i: (0, i)),
        out_shape=jax.ShapeDtypeStruct((h, p), F32),
        compiler_params=_cparams(("arbitrary",)),
        name="bias_expand",
    )(bucket, tab_t)


def _t5_bucket_np(dist):
    n = np.maximum(dist, 0)
    exact = REL_BUCKETS // 2
    log_ratio = np.log(np.maximum(n, 1).astype(np.float32) / exact) / math.log(REL_MAX_DIST / exact)
    large = np.minimum(exact + (log_ratio * (REL_BUCKETS - exact)).astype(np.int32), REL_BUCKETS - 1)
    return np.where(n < exact, n, large).astype(np.int32)


def _toeplitz_buckets(rows, cols, deltas):
    i = np.arange(rows)[:, None]
    j = np.arange(cols)[None, :]
    tiles = [_t5_bucket_np(d + i - j) for d in deltas]
    tiles.append(np.full((rows, cols), REL_BUCKETS - 1, np.int32))
    return np.stack(tiles).reshape(1, -1)


def _dsa_kernel(cq_ref, ckv_ref, kw_ref, cqg_ref, ckvg_ref, wuq_ref, wuk_ref, wqi_ref, wuv_ref, bias_ref,
                o_ref, kvn_ref, kk_ref, key_ref, key_t_ref, ql_ref, lg_ref, mx_ref, ls_ref, p_ref, acc_ref, *, seq, n_top):
    qb = pl.program_id(1)
    q0 = qb * DSA_QB
    n_chunk_rows = seq // DSA_KC

    @pl.when(qb == 0)
    def _():
        lane = lax.broadcasted_iota(I32, (DSA_KC, LANES), 1)

        def prep(c, carry):
            rows = pl.ds(pl.multiple_of(c * DSA_KC, DSA_KC), DSA_KC)
            ckv = ckv_ref[rows, :]
            ms = jnp.mean(ckv * ckv, axis=-1, keepdims=True)
            kvn_ref[rows, :] = (ckv * lax.rsqrt(ms + LN_EPS) * ckvg_ref[...]).astype(BF16)
            kw = kw_ref[rows, :]
            kk_ref[rows, :] = jnp.where(lane < IDX_DIM, kw, pltpu.roll(kw, IDX_DIM, 1)).astype(BF16)
            return carry

        lax.fori_loop(0, n_chunk_rows, prep, 0)

    nc = (q0 + DSA_QB + DSA_KC - 1) // DSA_KC

    cq = cq_ref[...]
    ms = jnp.mean(cq * cq, axis=-1, keepdims=True)
    cqn = (cq * lax.rsqrt(ms + LN_EPS) * cqg_ref[...]).astype(BF16)

    lane_q = lax.broadcasted_iota(I32, (DSA_QB, LANES), 1)
    lo_half = lane_q < IDX_DIM
    row_i = lax.broadcasted_iota(I32, (DSA_QB, DSA_KC), 0)
    col_j = lax.broadcasted_iota(I32, (DSA_QB, DSA_KC), 1)
    qpos = q0 + row_i

    qi = _dot(cqn, wqi_ref[...]) * (IDX_DIM ** -0.5)
    wq = kw_ref[pl.ds(pl.multiple_of(q0, DSA_QB), DSA_QB), :] * (IDX_HEADS ** -0.5)
    q_heads = []
    for h in range(IDX_HEADS):
        qp = qi[:, (h // 2) * LANES:(h // 2 + 1) * LANES]
        keep = lo_half if h % 2 == 0 else jnp.logical_not(lo_half)
        q_heads.append(jnp.where(keep, qp, 0.0).astype(BF16))

    def score_body(c, carry):
        kc = kk_ref[pl.ds(pl.multiple_of(c * DSA_KC, DSA_KC), DSA_KC), :]
        s = jnp.zeros((DSA_QB, DSA_KC), F32)
        for h in range(IDX_HEADS):
            r = _dot_nt(q_heads[h], kc)
            s = s + wq[:, IDX_DIM + h:IDX_DIM + h + 1] * jnp.maximum(r, 0.0)
        s = jnp.where(qpos >= c * DSA_KC + col_j, s, -jnp.inf)
        key_ref[c] = _order_key(s)
        key_t_ref[c] = _order_key(s.T).reshape(DSA_KC // SUBLANES, SUBLANES, DSA_QB)
        return carry

    lax.fori_loop(0, nc, score_body, 0)

    n_grp = DSA_KC // SUBLANES
    kpos_t = (lax.broadcasted_iota(I32, (n_grp, SUBLANES, DSA_QB), 0) * SUBLANES
              + lax.broadcasted_iota(I32, (n_grp, SUBLANES, DSA_QB), 1))

    def count(pred):
        def chunk(c):
            one = jnp.where(pred(key_t_ref[c], c), 1.0, 0.0)
            parts = [one[g] for g in range(n_grp)]
            while len(parts) > 1:
                parts = [parts[i] + parts[i + 1] for i in range(0, len(parts), 2)]
            return parts[0]

        tot = lax.fori_loop(0, nc, lambda c, tot: tot + chunk(c), jnp.zeros((SUBLANES, DSA_QB), F32))
        for d in (4, 2, 1):
            tot = tot + pltpu.roll(tot, d, 0)
        return tot

    int_min = jnp.int32(-2 ** 31)
    k_top = float(n_top)
    cnt = count(lambda k, c: k >= 0)
    thr0 = jnp.where(cnt >= k_top, jnp.int32(0), int_min)

    def bit_body(i, thr):
        cand = thr | jnp.left_shift(jnp.int32(1), 30 - i)
        cnt = count(lambda k, c: k >= cand)
        return jnp.where(cnt >= k_top, cand, thr)

    thr_t = lax.fori_loop(0, 31, bit_body, thr0)

    need = k_top - count(lambda k, c: k > thr_t)
    n_eq = count(lambda k, c: k == thr_t)
    any_tie = jnp.max(jnp.where(n_eq > need, 1.0, 0.0)) > 0.0
    idx_bits = int(math.log2(seq))

    def tie_cut():
        def body(i, v):
            cand = v | jnp.left_shift(jnp.int32(1), idx_bits - 1 - i)
            cnt = count(lambda k, c: (k == thr_t) & (c * DSA_KC + kpos_t < cand))
            return jnp.where(cnt < need, cand, v)

        return lax.fori_loop(0, idx_bits, body, jnp.zeros((SUBLANES, DSA_QB), I32))

    cut_t = lax.cond(any_tie, tie_cut, lambda: jnp.full((SUBLANES, DSA_QB), seq, I32))

    def to_rows(v_t):
        def col(x):
            return jnp.broadcast_to(x.astype(F32), (DSA_QB, DSA_QB)).T.astype(I32)

        v = v_t[0:1, :]
        r = (col(v >> 16) << 16) | col(v & 0xFFFF)
        return jnp.concatenate([r] * (DSA_KC // DSA_QB), axis=1)

    thr = to_rows(thr_t)
    cut = to_rows(cut_t)

    q = _dot(cqn, wuq_ref[...])
    for h in range(A_HEADS):
        qp = q[:, (h // 2) * LANES:(h // 2 + 1) * LANES]
        keep = lo_half if h % 2 == 0 else jnp.logical_not(lo_half)
        qm = jnp.where(keep, qp, 0.0).astype(BF16)
        ql_ref[h * DSA_QB:(h + 1) * DSA_QB, :] = (_dot(qm, wuk_ref[h // 2]) * (A_HEAD_DIM ** -0.5)).astype(BF16)
    mx_ref[...] = jnp.full(mx_ref.shape, NEG_INF, F32)
    ls_ref[...] = jnp.zeros(ls_ref.shape, F32)
    acc_ref[...] = jnp.zeros(acc_ref.shape, F32)

    def logit_body(c, carry):
        kv = kvn_ref[pl.ds(pl.multiple_of(c * DSA_KC, DSA_KC), DSA_KC), :]
        k = key_ref[c]
        kpos = c * DSA_KC + col_j
        sel = ((k > thr) | ((k == thr) & (kpos <= cut))) & (qpos >= kpos)
        add_mask = jnp.where(sel, 0.0, NEG_INF)
        t = jnp.minimum((q0 - c * DSA_KC) // DSA_QB, 3)
        s_all = _dot_nt(ql_ref[...], kv)
        for h in range(A_HEADS):
            rows = slice(h * DSA_QB, (h + 1) * DSA_QB)
            sm = s_all[rows, :] + bias_ref[t, h] + add_mask
            lg_ref[c, rows, :] = sm
            mx_ref[h] = jnp.maximum(mx_ref[h], jnp.maximum(sm[:, :LANES], sm[:, LANES:]))
        return carry

    _loop2(nc, lambda c: logit_body(c, 0))

    for h in range(A_HEADS):
        mx_ref[h] = jnp.broadcast_to(jnp.max(mx_ref[h], axis=1, keepdims=True), (DSA_QB, LANES))

    def pv_body(c, carry):
        kv = kvn_ref[pl.ds(pl.multiple_of(c * DSA_KC, DSA_KC), DSA_KC), :]
        for h in range(A_HEADS):
            rows = slice(h * DSA_QB, (h + 1) * DSA_QB)
            m = mx_ref[h]
            p_lo = jnp.exp(lg_ref[c, rows, :LANES] - m)
            p_hi = jnp.exp(lg_ref[c, rows, LANES:] - m)
            ls_ref[h] += p_lo + p_hi
            p_ref[rows, :LANES] = p_lo.astype(BF16)
            p_ref[rows, LANES:] = p_hi.astype(BF16)
        acc_ref[...] += _dot(p_ref[...], kv)
        return carry

    _loop2(nc, lambda c: pv_body(c, 0))

    for pr in range(A_HEADS // 2):
        o = jnp.zeros((DSA_QB, LANES), F32)
        for h in (2 * pr, 2 * pr + 1):
            l = jnp.sum(ls_ref[h], axis=1, keepdims=True)
            o_lat = (acc_ref[h * DSA_QB:(h + 1) * DSA_QB, :] / l).astype(BF16)
            o = o + _dot(o_lat, wuv_ref[h])
        o_ref[:, pr * LANES:(pr + 1) * LANES] = o.astype(o_ref.dtype)


def _dsa(hm, cq_gain, ckv_gain, w_uq, w_uk, w_uv, w_qidx, bias_tiles):
    bsz, seq, _ = hm.shape
    n_top = min(DSA_TOPK, seq // 4)
    hd = A_HEADS * A_HEAD_DIM
    wuq = w_uq.reshape(A_Q_RANK, hd).astype(BF16)
    wuk = jnp.transpose(w_uk, (1, 2, 0)).reshape(A_HEADS // 2, 2 * A_HEAD_DIM, A_KV_RANK).astype(BF16)
    wqi = w_qidx.reshape(A_Q_RANK, IDX_HEADS * IDX_DIM).astype(BF16)
    wv = jnp.transpose(w_uv, (1, 0, 2))
    zeros = jnp.zeros_like(wv)
    even = (jnp.arange(A_HEADS) % 2 == 0)[:, None, None]
    wuv = jnp.concatenate([jnp.where(even, wv, zeros), jnp.where(even, zeros, wv)], axis=-1).astype(BF16)
    kernel = functools.partial(_dsa_kernel, seq=seq, n_top=n_top)
    const2 = lambda b, q: (0, 0)
    const3 = lambda b, q: (0, 0, 0)
    return pl.pallas_call(
        kernel,
        grid=(bsz, seq // DSA_QB),
        in_specs=[
            pl.BlockSpec((None, DSA_QB, A_Q_RANK), lambda b, q: (b, q, CQ_OFF // A_Q_RANK)),
            pl.BlockSpec((None, seq, A_KV_RANK), lambda b, q: (b, 0, CKV_OFF // A_KV_RANK)),
            pl.BlockSpec((None, seq, LANES), lambda b, q: (b, 0, KW_OFF // LANES)),
            pl.BlockSpec((1, A_Q_RANK), const2),
            pl.BlockSpec((1, A_KV_RANK), const2),
            pl.BlockSpec(wuq.shape, const2),
            pl.BlockSpec(wuk.shape, const3),
            pl.BlockSpec(wqi.shape, const2),
            pl.BlockSpec(wuv.shape, const3),
            pl.BlockSpec(bias_tiles.shape, lambda b, q: (0, 0, 0, 0)),
        ],
        out_specs=pl.BlockSpec((None, DSA_QB, hd), lambda b, q: (b, q, 0)),
        out_shape=jax.ShapeDtypeStruct((bsz, seq, hd), BF16),
        scratch_shapes=[
            pltpu.VMEM((seq, A_KV_RANK), BF16),
            pltpu.VMEM((seq, LANES), BF16),
            pltpu.VMEM((seq // DSA_KC, DSA_QB, DSA_KC), I32),
            pltpu.VMEM((seq // DSA_KC, DSA_KC // SUBLANES, SUBLANES, DSA_QB), I32),
            pltpu.VMEM((A_HEADS * DSA_QB, A_KV_RANK), BF16),
            pltpu.VMEM((seq // DSA_KC, A_HEADS * DSA_QB, DSA_KC), F32),
            pltpu.VMEM((A_HEADS, DSA_QB, LANES), F32),
            pltpu.VMEM((A_HEADS, DSA_QB, LANES), F32),
            pltpu.VMEM((A_HEADS * DSA_QB, DSA_KC), BF16),
            pltpu.VMEM((A_HEADS * DSA_QB, A_KV_RANK), F32),
        ],
        compiler_params=_cparams(("arbitrary", "arbitrary")),
        name="dsa",
    )(hm, hm, hm, cq_gain.reshape(1, -1), ckv_gain.reshape(1, -1), wuq, wuk, wqi, wuv, bias_tiles)


S5_POWERS = (1,)
S5_ROWS = 2 * len(S5_POWERS) + 2


def _s5_prep_kernel(lre_ref, lim_ref, ls_ref, bre_ref, bim_ref, tab_ref, bbre_ref, bbim_ref):
    lre = lre_ref[...]
    lim = lim_ref[...]
    step = jnp.exp(ls_ref[...])
    for r, k in enumerate(S5_POWERS):
        mag = jnp.exp(k * lre * step)
        ang = k * lim * step
        tab_ref[2 * r:2 * r + 1, :] = mag * jnp.cos(ang)
        tab_ref[2 * r + 1:2 * r + 2, :] = mag * jnp.sin(ang)
    a = tab_ref[0:1, :] - 1.0
    b = tab_ref[1:2, :]
    den = lre * lre + lim * lim
    f_re = (a * lre + b * lim) / den
    f_im = (b * lre - a * lim) / den
    base = 2 * len(S5_POWERS)
    tab_ref[base:base + 1, :] = f_re
    tab_ref[base + 1:base + 2, :] = f_im
    bre = bre_ref[...]
    bim = bim_ref[...]
    bbre_ref[...] = f_re * bre - f_im * bim
    bbim_ref[...] = f_re * bim + f_im * bre


def _s5_prep(lam_re, lam_im, log_step, b_re, b_im):
    lre = lam_re.reshape(1, NS)
    lim = lam_im.reshape(1, NS)
    ls = jnp.repeat(log_step, SSM_STATE).reshape(1, NS)
    bre = jnp.transpose(b_re, (2, 0, 1)).reshape(SSM_GROUP, NS)
    bim = jnp.transpose(b_im, (2, 0, 1)).reshape(SSM_GROUP, NS)
    return pl.pallas_call(
        _s5_prep_kernel,
        out_shape=(jax.ShapeDtypeStruct((S5_ROWS, NS), F32),
                   jax.ShapeDtypeStruct((SSM_GROUP, NS), F32),
                   jax.ShapeDtypeStruct((SSM_GROUP, NS), F32)),
        name="s5_prep",
    )(lre, lim, ls, bre, bim)


def _s5_kernel(u_ref, perm_ref, perm_t_ref, bm_ref, cm_ref, lam_ref, dsk_ref, wg_ref, bg_ref, o_ref, xs_ref, c_ref,
               *, lane_chunk, unroll):
    tc = pl.program_id(1)
    rows_all = SUBLANES * S5_LT

    @pl.when(tc == 0)
    def _():
        c_ref[...] = jnp.zeros(c_ref.shape, F32)

    u = u_ref[...].reshape(rows_all, SSM_WIDTH)
    u_tm = _dot(perm_ref[...], u.astype(BF16)).astype(BF16)
    n_sg = SSM_WIDTH // LANES
    sg_w = NS // n_sg
    for sg in range(n_sg):
        bu = _dot(u_tm[:, sg * LANES:(sg + 1) * LANES], bm_ref[sg])
        xs_ref[:, sg * sg_w:(sg + 1) * sg_w] = bu[:, :sg_w]
        xs_ref[:, NS + sg * sg_w:NS + (sg + 1) * sg_w] = bu[:, sg_w:]

    for lc in range(NS // lane_chunk):
        re_cols = slice(lc * lane_chunk, (lc + 1) * lane_chunk)
        im_cols = slice(NS + lc * lane_chunk, NS + (lc + 1) * lane_chunk)
        l_re = lam_ref[0, :, re_cols]
        l_im = lam_ref[1, :, re_cols]

        def step_body(i, carry, re_cols=re_cols, im_cols=im_cols, l_re=l_re, l_im=l_im):
            x_re, x_im = carry
            for k in range(unroll):
                rows = pl.ds(pl.multiple_of((i * unroll + k) * SUBLANES, SUBLANES), SUBLANES)
                x_re, x_im = (l_re * x_re - l_im * x_im + xs_ref[rows, re_cols],
                              l_re * x_im + l_im * x_re + xs_ref[rows, im_cols])
                xs_ref[rows, re_cols] = x_re
                xs_ref[rows, im_cols] = x_im
            return x_re, x_im

        x_re, x_im = lax.fori_loop(0, S5_LT // unroll, step_body, (c_ref[:, re_cols], c_ref[:, im_cols]))
        c_ref[:, re_cols] = x_re
        c_ref[:, im_cols] = x_im

    y_tm = jnp.concatenate(
        [_dot(xs_ref[:, sg * sg_w:(sg + 1) * sg_w].astype(BF16), cm_ref[sg, :sg_w, :])
         + _dot(xs_ref[:, NS + sg * sg_w:NS + (sg + 1) * sg_w].astype(BF16), cm_ref[sg, sg_w:, :])
         for sg in range(n_sg)], axis=1)
    y1 = y_tm.astype(BF16)
    r1 = y_tm - y1.astype(F32)
    y2 = r1.astype(BF16)
    y3 = (r1 - y2.astype(F32)).astype(BF16)
    pt = perm_t_ref[...]
    y = _dot(pt, y1) + (_dot(pt, y2) + _dot(pt, y3)) + dsk_ref[...] * u
    y = jax.nn.gelu(y)
    z = _dot(y.astype(BF16), wg_ref[...]) + bg_ref[...]
    o_ref[...] = (y * jax.nn.sigmoid(z)).astype(o_ref.dtype).reshape(o_ref.shape)


def _s5(hm, tab, bb_re, bb_im, c_re, c_im, d_skip, w_glu, b_glu, lane_chunk=1024):
    bsz, seq, _ = hm.shape
    eye = jnp.eye(SSM_GROUPS, dtype=F32)

    def in_map(bb):
        bb = jnp.transpose(bb.reshape(SSM_GROUP, SSM_GROUPS, SSM_STATE), (1, 0, 2))
        return jnp.einsum('gpn,gh->gphn', bb, eye).reshape(SSM_WIDTH, NS)

    def out_map(c):
        return jnp.einsum('gpn,gh->gnhp', c, eye).reshape(NS, SSM_WIDTH)

    n_sg = SSM_WIDTH // LANES
    sg_w = NS // n_sg

    def diag(m, rows, cols):
        return jnp.stack([m[s * rows:(s + 1) * rows, s * cols:(s + 1) * cols] for s in range(n_sg)])

    bm = jnp.concatenate([diag(in_map(bb_re), LANES, sg_w), diag(in_map(bb_im), LANES, sg_w)], axis=2).astype(BF16)
    cm = jnp.concatenate([diag(out_map(c_re), sg_w, LANES), -diag(out_map(c_im), sg_w, LANES)], axis=1).astype(BF16)
    lam = jnp.broadcast_to(tab[0:2, None, :], (2, SUBLANES, NS))
    rows_all = SUBLANES * S5_LT
    r = np.arange(rows_all)
    perm = np.zeros((rows_all, rows_all), np.float32)
    perm[r, (r % SUBLANES) * S5_LT + r // SUBLANES] = 1.0
    perm_t = jnp.asarray(perm.T, BF16)
    perm = jnp.asarray(perm, BF16)
    assert bsz % SUBLANES == 0 and seq % S5_LT == 0
    kernel = functools.partial(_s5_kernel, lane_chunk=lane_chunk, unroll=4)
    const2 = lambda b, t: (0, 0)
    once = dict(pipeline_mode=pl.Buffered(1))
    return pl.pallas_call(
        kernel,
        grid=(bsz // SUBLANES, seq // S5_LT),
        in_specs=[
            pl.BlockSpec((SUBLANES, S5_LT, SSM_WIDTH), lambda b, t: (b, t, U_OFF // SSM_WIDTH)),
            pl.BlockSpec(perm.shape, const2, **once),
            pl.BlockSpec(perm.shape, const2, **once),
            pl.BlockSpec(bm.shape, lambda b, t: (0, 0, 0), **once),
            pl.BlockSpec(cm.shape, lambda b, t: (0, 0, 0), **once),
            pl.BlockSpec(lam.shape, lambda b, t: (0, 0, 0), **once),
            pl.BlockSpec((1, SSM_WIDTH), const2),
            pl.BlockSpec((SSM_WIDTH, SSM_WIDTH), const2),
            pl.BlockSpec((1, SSM_WIDTH), const2),
        ],
        out_specs=pl.BlockSpec((SUBLANES, S5_LT, SSM_WIDTH), lambda b, t: (b, t, 0)),
        out_shape=jax.ShapeDtypeStruct((bsz, seq, SSM_WIDTH), BF16),
        scratch_shapes=[pltpu.VMEM((rows_all, 2 * NS), F32),
                        pltpu.VMEM((SUBLANES, 2 * NS), F32)],
        compiler_params=_cparams(("arbitrary", "arbitrary")),
        name="s5",
    )(hm, perm, perm_t, bm, cm, lam, d_skip.reshape(1, -1), w_glu.astype(BF16), b_glu.reshape(1, -1))


def _moba_kernel(q_ref, k_ref, v_ref, bias_ref, o_ref, kb_ref, vb_ref, km_ref, lg_ref, mx_ref, ls_ref, p_ref, acc_ref,
                 *, n_blk):
    qb = pl.program_id(2)
    blk = MOBA_BLOCK
    heads = range(2 * MOBA_PAIRS)

    def pair_lanes(pr):
        return slice(pr * LANES, (pr + 1) * LANES)

    @pl.when(qb == 0)
    def _():
        km_ref[...] = jnp.zeros(km_ref.shape, F32)
        lane_k = lax.broadcasted_iota(I32, (blk, LANES), 1)
        for pr in range(MOBA_PAIRS):
            for n in range(n_blk):
                rows = slice(n * blk, (n + 1) * blk)
                kblk = k_ref[rows, pair_lanes(pr)]
                km_ref[pr, n:n + 1, :] = jnp.mean(kblk, axis=0, keepdims=True)
                kb_ref[2 * pr, rows, :] = jnp.where(lane_k < C_HEAD_DIM, kblk,
                                                    jnp.where(lane_k == C_HEAD_DIM + n, 1.0, 0.0)).astype(BF16)
                kb_ref[2 * pr + 1, rows, :] = jnp.where(lane_k >= C_HEAD_DIM, kblk,
                                                        jnp.where(lane_k == n, 1.0, 0.0)).astype(BF16)
                vb_ref[pr, rows, :] = v_ref[rows, pair_lanes(pr)].astype(BF16)

    lane = lax.broadcasted_iota(I32, (blk, LANES), 1)
    row_i = lax.broadcasted_iota(I32, (blk, blk), 0)
    col_j = lax.broadcasted_iota(I32, (blk, blk), 1)
    causal_add = jnp.where(row_i >= col_j, 0.0, NEG_INF)
    blk_id = lax.broadcasted_iota(I32, (SUBLANES, blk), 0)
    blk_f = blk_id.astype(F32)
    lane_m = lax.broadcasted_iota(I32, (SUBLANES, LANES), 1)
    scale = C_HEAD_DIM ** -0.5
    own_rows = pl.ds(pl.multiple_of(qb * blk, blk), blk)

    def over_blocks(x, op):
        for d in (4, 2, 1):
            x = op(x, pltpu.roll(x, d, 0))
        return x

    qa = []
    for pr in range(MOBA_PAIRS):
        q = q_ref[:, pair_lanes(pr)]
        km = km_ref[pr]
        km_both = jnp.concatenate([jnp.where(lane_m < C_HEAD_DIM, km, 0.0), jnp.where(lane_m >= C_HEAD_DIM, km, 0.0)],
                                  axis=0)
        km_hi = km_both.astype(BF16)
        km_lo = (km_both - km_hi.astype(F32)).astype(BF16)
        q_hi = q.astype(BF16)
        q_lo = (q - q_hi.astype(F32)).astype(BF16)
        gate_both = _dot_nt(km_hi, q_hi) + (_dot_nt(km_hi, q_lo) + _dot_nt(km_lo, q_hi))
        for s in range(2):
            hd = 2 * pr + s
            keep = (lane < C_HEAD_DIM) if s == 0 else (lane >= C_HEAD_DIM)
            gate = gate_both[s * SUBLANES:(s + 1) * SUBLANES, :]
            g = jnp.where(blk_id < qb, gate, -jnp.inf)
            sel = jnp.zeros((SUBLANES, blk), F32)
            for _ in range(MOBA_TOPK):
                mx = over_blocks(g, jnp.maximum)
                is_max = (g == mx) & (g > -jnp.inf)
                first = over_blocks(jnp.where(is_max, blk_f, float(SUBLANES)), jnp.minimum)
                pick = blk_f == first
                sel = jnp.where(pick, 1.0, sel)
                g = jnp.where(pick, -jnp.inf, g)
            add_t = jnp.where((sel > 0.5) | (blk_id == qb), 0.0, NEG_INF)
            lead = C_HEAD_DIM if s == 0 else 0
            pieces = ([jnp.zeros((lead, blk), F32)] if lead else []) + [add_t]
            pieces.append(jnp.zeros((LANES - lead - SUBLANES, blk), F32))
            add_q = jnp.concatenate(pieces, axis=0).T
            qa.append(jnp.where(keep, q * scale, add_q).astype(BF16))
            sm = _dot_nt(qa[hd], kb_ref[hd, own_rows, :]) + bias_ref[0, hd] + causal_add
            lg_ref[hd, qb] = sm
            mx_ref[hd] = jnp.maximum(sm[:, :LANES], sm[:, LANES:])
            ls_ref[hd] = jnp.zeros((blk, LANES), F32)
            acc_ref[hd] = jnp.zeros((blk, LANES), F32)

    def logit_body(n):
        rows = pl.ds(pl.multiple_of(n * blk, blk), blk)
        t = jnp.minimum(qb - n, 2)
        for hd in heads:
            sm = _dot_nt(qa[hd], kb_ref[hd, rows, :]) + bias_ref[t, hd]
            lg_ref[hd, n] = sm
            mx_ref[hd] = jnp.maximum(mx_ref[hd], jnp.maximum(sm[:, :LANES], sm[:, LANES:]))

    _loop2(qb, logit_body)

    for hd in heads:
        mx_ref[hd] = jnp.broadcast_to(jnp.max(mx_ref[hd], axis=1, keepdims=True), (blk, LANES))

    def pv_body(n):
        rows = pl.ds(pl.multiple_of(n * blk, blk), blk)
        for hd in heads:
            m = mx_ref[hd]
            p_lo = jnp.exp(lg_ref[hd, n, :, :LANES] - m)
            p_hi = jnp.exp(lg_ref[hd, n, :, LANES:] - m)
            ls_ref[hd] += p_lo + p_hi
            p_ref[hd, :, :LANES] = p_lo.astype(BF16)
            p_ref[hd, :, LANES:] = p_hi.astype(BF16)
            acc_ref[hd] += _dot(p_ref[hd], vb_ref[hd // 2, rows, :])

    _loop2(qb + 1, pv_body)

    for pr in range(MOBA_PAIRS):
        o = jnp.where(lane < C_HEAD_DIM,
                      acc_ref[2 * pr] / jnp.sum(ls_ref[2 * pr], axis=1, keepdims=True),
                      acc_ref[2 * pr + 1] / jnp.sum(ls_ref[2 * pr + 1], axis=1, keepdims=True))
        o_ref[:, pair_lanes(pr)] = o.astype(o_ref.dtype)


def _moba(hm, bias_tiles):
    bsz, seq, _ = hm.shape
    n_blk = seq // MOBA_BLOCK
    hs = 2 * MOBA_PAIRS
    wide = MOBA_PAIRS * LANES
    n_grp = C_HEADS // hs
    assert n_blk <= SUBLANES
    kernel = functools.partial(_moba_kernel, n_blk=n_blk)
    return pl.pallas_call(
        kernel,
        grid=(bsz, n_grp, n_blk),
        in_specs=[
            pl.BlockSpec((None, MOBA_BLOCK, wide), lambda b, h, q: (b, q, Q_OFF // wide + h)),
            pl.BlockSpec((None, seq, wide), lambda b, h, q: (b, 0, K_OFF // wide + h)),
            pl.BlockSpec((None, seq, wide), lambda b, h, q: (b, 0, V_OFF // wide + h)),
            pl.BlockSpec((3, hs, MOBA_BLOCK, MOBA_BLOCK), lambda b, h, q: (0, h, 0, 0)),
        ],
        out_specs=pl.BlockSpec((None, MOBA_BLOCK, wide), lambda b, h, q: (b, q, h)),
        out_shape=jax.ShapeDtypeStruct((bsz, seq, C_HEADS * C_HEAD_DIM), BF16),
        scratch_shapes=[
            pltpu.VMEM((hs, seq, LANES), BF16),
            pltpu.VMEM((MOBA_PAIRS, seq, LANES), BF16),
            pltpu.VMEM((MOBA_PAIRS, SUBLANES, LANES), F32),
            pltpu.VMEM((hs, n_blk, MOBA_BLOCK, MOBA_BLOCK), F32),
            pltpu.VMEM((hs, MOBA_BLOCK, LANES), F32),
            pltpu.VMEM((hs, MOBA_BLOCK, LANES), F32),
            pltpu.VMEM((hs, MOBA_BLOCK, MOBA_BLOCK), BF16),
            pltpu.VMEM((hs, MOBA_BLOCK, LANES), F32),
        ],
        compiler_params=_cparams(("arbitrary", "arbitrary", "arbitrary")),
        name="moba",
    )(hm, hm, hm, bias_tiles)


def _merge_kernel(xb_ref, oa_ref, ob_ref, oc_ref, wg0_ref, wg1_ref, wg2_ref, wb_ref, o_ref):
    xb = xb_ref[...].astype(BF16)
    merged = None
    for n, (o_n, wg_n) in enumerate(((oa_ref, wg0_ref), (ob_ref, wg1_ref), (oc_ref, wg2_ref))):
        gate = jax.nn.sigmoid(_dot(xb, wg_n[...]))
        y = _dot(o_n[...], wb_ref[n])
        merged = gate * y if merged is None else merged + gate * y
    o_ref[...] = merged.astype(BF16)


def _merge(xb, oa, ob, oc, wg, wb, tm=512, tn=1024):
    n, d = xb.shape
    tm = min(tm, n)
    nd = d // tn
    bw = oa.shape[1]
    row = lambda j, i: (i, 0)
    once = dict(pipeline_mode=pl.Buffered(1))
    return pl.pallas_call(
        _merge_kernel,
        grid=(nd, n // tm),
        in_specs=[
            pl.BlockSpec((tm, d), row),
            pl.BlockSpec((tm, bw), row),
            pl.BlockSpec((tm, bw), row),
            pl.BlockSpec((tm, bw), row),
            pl.BlockSpec((d, tn), lambda j, i: (0, j), **once),
            pl.BlockSpec((d, tn), lambda j, i: (0, nd + j), **once),
            pl.BlockSpec((d, tn), lambda j, i: (0, 2 * nd + j), **once),
            pl.BlockSpec((N_BRANCH, bw, tn), lambda j, i: (0, 0, j), **once),
        ],
        out_specs=pl.BlockSpec((tm, tn), lambda j, i: (i, j)),
        out_shape=jax.ShapeDtypeStruct((n, d), BF16),
        compiler_params=_cparams(("arbitrary", "arbitrary")),
        name="merge",
    )(xb, oa, ob, oc, wg, wg, wg, wb)


def _ffn_up_kernel(x_ref, wa_ref, wv_ref, cwa_ref, cwv_ref, cba_ref, cbv_ref, o_ref, tail_ref,
                   *, tiles_per_seq):
    i = pl.program_id(1)
    tm = x_ref.shape[0]

    @pl.when(i % tiles_per_seq == 0)
    def _():
        tail_ref[...] = jnp.zeros(tail_ref.shape, F32)

    last = SUBLANES - 1
    row = lax.broadcasted_iota(I32, (SUBLANES, o_ref.shape[1]), 0)

    def conv(part, xb, w_ref, cw_ref, cb_ref):
        h = _dot(xb, w_ref[...])
        hh = tail_ref[part]
        tail_ref[part] = h[tm - SUBLANES:tm, :]
        w0, w1, w2, cb = cw_ref[0:1, :], cw_ref[1:2, :], cw_ref[2:3, :], cb_ref[...]
        body = cb + w0 * pltpu.roll(h, 2, 0) + w1 * pltpu.roll(h, 1, 0) + w2 * h
        h8 = h[0:SUBLANES, :]
        p1 = jnp.where(row == 0, hh[last:last + 1, :], pltpu.roll(h8, 1, 0))
        p2 = jnp.where(row == 0, hh[last - 1:last, :], jnp.where(row == 1, hh[last:last + 1, :], pltpu.roll(h8, 2, 0)))
        head = cb + w0 * p2 + w1 * p1 + w2 * h8
        return jnp.concatenate([head, body[SUBLANES:, :]], axis=0)

    xb = x_ref[...]
    a = conv(0, xb, wa_ref, cwa_ref, cba_ref)
    val = conv(1, xb, wv_ref, cwv_ref, cbv_ref)
    o_ref[...] = (jax.nn.gelu(a) * val).astype(o_ref.dtype)


def _ffn_up(xb, seq, w_up, conv_w, conv_b, tm=512, tn=2816):
    n, d = xb.shape
    d_ff = w_up.shape[1] // 2
    tm = min(tm, seq)
    nf = d_ff // tn
    assert nf * tn == d_ff and tn % LANES == 0
    kernel = functools.partial(_ffn_up_kernel, tiles_per_seq=seq // tm)
    return pl.pallas_call(
        kernel,
        grid=(nf, n // tm),
        in_specs=[
            pl.BlockSpec((tm, d), lambda j, i: (i, 0)),
            pl.BlockSpec((d, tn), lambda j, i: (0, j), pipeline_mode=pl.Buffered(1)),
            pl.BlockSpec((d, tn), lambda j, i: (0, nf + j), pipeline_mode=pl.Buffered(1)),
            pl.BlockSpec((CONV_WIDTH, tn), lambda j, i: (0, j)),
            pl.BlockSpec((CONV_WIDTH, tn), lambda j, i: (0, nf + j)),
            pl.BlockSpec((1, tn), lambda j, i: (0, j)),
            pl.BlockSpec((1, tn), lambda j, i: (0, nf + j)),
        ],
        out_specs=pl.BlockSpec((tm, tn), lambda j, i: (i, j)),
        out_shape=jax.ShapeDtypeStruct((n, d_ff), BF16),
        scratch_shapes=[pltpu.VMEM((2, SUBLANES, tn), F32)],
        compiler_params=_cparams(("arbitrary", "arbitrary")),
        name="ffn_up",
    )(xb, w_up, w_up, conv_w, conv_w, conv_b.reshape(1, -1), conv_b.reshape(1, -1))


def _resid_proj_kernel(a_ref, w_ref, x_ref, g_ref, b_ref, o_ref, o16_ref):
    half = o_ref.shape[0] // 2
    for rows in (slice(0, half), slice(half, 2 * half)):
        z = DEEPNORM_ALPHA * x_ref[rows, :] + _dot(a_ref[rows, :], w_ref[...])
        y = _layer_norm(z, g_ref[...], b_ref[...])
        o_ref[rows, :] = y
        o16_ref[rows, :] = y.astype(BF16)


def _resid_proj(act, w, x2d, ln_g, ln_b, tm=256):
    n, d = x2d.shape
    k = act.shape[1]
    tm = min(tm, n)
    row = lambda i: (i, 0)
    const2 = lambda i: (0, 0)
    return pl.pallas_call(
        _resid_proj_kernel,
        grid=(n // tm,),
        in_specs=[
            pl.BlockSpec((tm, k), row),
            pl.BlockSpec((k, d), const2, pipeline_mode=pl.Buffered(1)),
            pl.BlockSpec((tm, d), row),
            pl.BlockSpec((1, d), const2),
            pl.BlockSpec((1, d), const2),
        ],
        out_specs=[pl.BlockSpec((tm, d), row), pl.BlockSpec((tm, d), row)],
        out_shape=[jax.ShapeDtypeStruct((n, d), F32), jax.ShapeDtypeStruct((n, d), BF16)],
        compiler_params=_cparams(("arbitrary",)),
        name="resid_proj",
    )(act, w, x2d, ln_g.reshape(1, -1), ln_b.reshape(1, -1))


def _mixer_weight(w_in_l):
    d = w_in_l.shape[0]
    c_width = C_HEADS * C_HEAD_DIM
    sizes = (A_Q_RANK, A_KV_RANK, IDX_DIM, IDX_HEADS, SSM_WIDTH, c_width, c_width, c_width)
    offs = np.cumsum((0,) + sizes)
    seg = [w_in_l[:, offs[i]:offs[i + 1]] for i in range(len(sizes))]
    pad = lambda w: jnp.zeros((d, w), w_in_l.dtype)
    cols = [seg[0], pad(CKV_OFF - A_Q_RANK), seg[1], seg[2], seg[3], pad(U_OFF - KW_OFF - IDX_DIM - IDX_HEADS),
            seg[4], seg[5], seg[6], seg[7]]
    w = jnp.concatenate(cols, axis=1)
    assert w.shape[1] == PROJ_W
    return w.astype(BF16), w_in_l[:, offs[-1]:].astype(BF16)


def kernel(x, rel_bias, w_in, cq_gain, ckv_gain, w_uq, w_uk, w_uv, w_qidx, lam_re, lam_im, log_step, b_re, b_im, c_re, c_im, d_skip, w_glu, b_glu, w_branch, w_out, ln1_g, ln1_b, w_up, conv_w, conv_b, w_down, ln2_g, ln2_b):
    bsz, seq, d = x.shape
    assert seq % MOBA_BLOCK == 0 and seq % DSA_KC == 0
    n = bsz * seq

    tab_t = jnp.transpose(rel_bias)
    dsa_bkt = jnp.asarray(_toeplitz_buckets(DSA_QB, DSA_KC, (0, DSA_QB, 2 * DSA_QB)))
    moba_bkt = jnp.asarray(_toeplitz_buckets(MOBA_BLOCK, MOBA_BLOCK, (0, MOBA_BLOCK)))
    dsa_bias = _bias_expand(dsa_bkt, tab_t[:A_HEADS]).reshape(A_HEADS, 4, DSA_QB, DSA_KC)
    dsa_bias = jnp.transpose(dsa_bias, (1, 0, 2, 3))
    moba_bias = _bias_expand(moba_bkt, tab_t[A_HEADS:]).reshape(C_HEADS, 3, MOBA_BLOCK, MOBA_BLOCK)
    moba_bias = jnp.transpose(moba_bias, (1, 0, 2, 3))

    x2d = x.reshape(n, d)
    xb = x2d
    for l in range(DEPTH):
        w_mix, w_gate = _mixer_weight(w_in[l])
        hm = _proj(xb, w_mix).reshape(bsz, seq, PROJ_W)
        o_a = _dsa(hm, cq_gain[l], ckv_gain[l], w_uq[l], w_uk[l], w_uv[l], w_qidx[l], dsa_bias)
        tab, bb_re, bb_im = _s5_prep(lam_re[l], lam_im[l], log_step[l], b_re[l], b_im[l])
        o_b = _s5(hm, tab, bb_re, bb_im, c_re[l], c_im[l], d_skip[l], w_glu[l], b_glu[l])
        o_c = _moba(hm, moba_bias)
        bw = o_a.shape[-1]
        merged = _merge(xb, o_a.reshape(n, bw), o_b.reshape(n, bw), o_c.reshape(n, bw), w_gate,
                        w_branch[l].astype(BF16))
        x2d, xb = _resid_proj(merged, w_out[l].astype(BF16), x2d, ln1_g[l], ln1_b[l], tm=512)
        act = _ffn_up(xb, seq, w_up[l].astype(BF16), conv_w[l], conv_b[l])
        x2d, xb = _resid_proj(act, w_down[l].astype(BF16), x2d, ln2_g[l], ln2_b[l])
    return x2d.reshape(bsz, seq, d)
```
